```python
import jax, jax.numpy as jnp
from jax import lax
import numpy as np

D_MODEL = 1024
BATCH = 4
SEQ = 4096
DEPTH = 4
DEC_BATCH = 16
DEC_SEQ = 2048
PAST_LEN = 128

N_MIXERS = 3
RMS_EPS = 1e-6
D_FF = ((8 * D_MODEL // 3 + 255) // 256) * 256

GRID_W = 64
NA_HEADS = 16
NA_HEAD_DIM = D_MODEL // NA_HEADS
NA_KH_MAX = 8
NA_KW = 16
NA_QBW = 16
NA_KBW = NA_QBW + NA_KW
NA_NCB = GRID_W // NA_QBW

LRU_WIDTH = ((4 * D_MODEL // 3 + 127) // 128) * 128
LRU_BLOCKS = 16
LRU_BLOCK = LRU_WIDTH // LRU_BLOCKS
CONV_W = 4
CONV_PAD = (2, 1)
LRU_C = 8.0

MLA_HEADS = 16
MLA_Q_RANK = 384
MLA_KV_RANK = 256
MLA_NOPE = 64
MLA_ROPE = 32
MLA_V = 64
ROPE_THETA = 10000.0
Q_BLOCK = 128

kernel_name = "hybrid_na_rglru_mla_encoder"


def rms_norm(x, g):
    xf = x.astype(jnp.float32)
    y = xf * lax.rsqrt(jnp.mean(xf * xf, axis=-1, keepdims=True) + RMS_EPS)
    return (y * g.astype(jnp.float32)).astype(x.dtype)


def swiglu(x, w_gate, w_up, w_down):
    return (jax.nn.silu(x @ w_gate) * (x @ w_up)) @ w_down


def _na_tables():
    qc = np.arange(GRID_W)
    win_start = np.clip(qc - NA_KW // 2, 0, GRID_W - NA_KW)
    band_start = np.clip(np.arange(NA_NCB) * NA_QBW - NA_KW // 2, 0, GRID_W - NA_KBW)
    key_cols = band_start[:, None] + np.arange(NA_KBW)
    qcols = qc.reshape(NA_NCB, NA_QBW)
    kc = key_cols[:, None, :]
    ws = win_start[qcols][:, :, None]
    valid = (kc >= ws) & (kc < ws + NA_KW)
    dc_idx = np.clip(kc - qcols[:, :, None] + NA_KW - 1, 0, 2 * NA_KW - 2)
    return key_cols, valid, dc_idx


def neighbourhood_attention(x, w_qkv, rpb, w_o):
    B, T, _ = x.shape
    rows = T // GRID_W
    kh = min(NA_KH_MAX, rows)
    key_cols, valid, dc_idx = _na_tables()
    qkv = (x @ w_qkv).reshape(B, rows, GRID_W, 3, NA_HEADS, NA_HEAD_DIM)
    q = qkv[:, :, :, 0].reshape(B, rows, NA_NCB, NA_QBW, NA_HEADS, NA_HEAD_DIM)
    k = jnp.take(qkv[:, :, :, 1], jnp.asarray(key_cols), axis=2)
    v = jnp.take(qkv[:, :, :, 2], jnp.asarray(key_cols), axis=2)
    valid_j = jnp.asarray(valid)[:, :, None, :]
    dc_j = jnp.asarray(dc_idx)
    scale = NA_HEAD_DIM ** -0.5

    def row_step(r):
        rs = jnp.clip(r - kh // 2, 0, rows - kh)
        kr = lax.dynamic_slice_in_dim(k, rs, kh, axis=1)
        vr = lax.dynamic_slice_in_dim(v, rs, kh, axis=1)
        qr = lax.dynamic_index_in_dim(q, r, axis=1, keepdims=False)
        s = jnp.einsum('bjqhd,bkjchd->bhjqkc', qr, kr).astype(jnp.float32) * scale
        dr_idx = rs + jnp.arange(kh) - r + NA_KH_MAX - 1
        bias = jnp.transpose(rpb[:, dr_idx][:, :, dc_j], (0, 2, 3, 1, 4))
        s = jnp.where(valid_j, s + bias.astype(jnp.float32), -jnp.inf)
        p = jax.nn.softmax(s.reshape(B, NA_HEADS, NA_NCB, NA_QBW, kh * NA_KBW), axis=-1)
        p = p.reshape(s.shape).astype(x.dtype)
        return jnp.einsum('bhjqkc,bkjchd->bjqhd', p, vr)

    o = lax.map(row_step, jnp.arange(rows))
    o = jnp.moveaxis(o, 0, 1).reshape(B, T, D_MODEL)
    return o @ w_o


def _lin_comb(e1, e2):
    a1, b1 = e1
    a2, b2 = e2
    return a1 * a2, a2 * b1 + b2


def rglru_direction(xc, w_a, b_a, w_x, b_x, lam, reverse):
    B, T, C = xc.shape
    xb = xc.reshape(B, T, LRU_BLOCKS, LRU_BLOCK)
    r = jax.nn.sigmoid((jnp.einsum('btnc,ncd->btnd', xb, w_a).reshape(B, T, C) + b_a).astype(jnp.float32))
    i = jax.nn.sigmoid((jnp.einsum('btnc,ncd->btnd', xb, w_x).reshape(B, T, C) + b_x).astype(jnp.float32))
    log_a = -LRU_C * r * jax.nn.softplus(-lam.astype(jnp.float32))
    a = jnp.exp(log_a)
    u = jnp.sqrt(-jnp.expm1(2.0 * log_a)) * (i * xc.astype(jnp.float32))
    _, h = lax.associative_scan(_lin_comb, (a, u), axis=1, reverse=reverse)
    return h


def recurrent_block(x, w_in, conv_w, conv_b, w_a, b_a, w_x, b_x, lam, w_out):
    gate, xr = jnp.split(x @ w_in, 2, axis=-1)
    xc = lax.conv_general_dilated(xr, conv_w[:, None, :], window_strides=(1,), padding=[CONV_PAD],
                                  dimension_numbers=('NWC', 'WIO', 'NWC'),
                                  feature_group_count=LRU_WIDTH) + conv_b
    h = (rglru_direction(xc, w_a[0], b_a[0], w_x[0], b_x[0], lam[0], False)
         + rglru_direction(xc, w_a[1], b_a[1], w_x[1], b_x[1], lam[1], True))
    y = jax.nn.gelu(gate.astype(jnp.float32)) * h
    return y.astype(x.dtype) @ w_out


def apply_rope(x, cos, sin):
    x1, x2 = jnp.split(x.astype(jnp.float32), 2, axis=-1)
    return jnp.concatenate([x1 * cos - x2 * sin, x1 * sin + x2 * cos], axis=-1).astype(x.dtype)


def mla(x, w_dq, g_q, w_uq, w_dkv, g_kv, w_ukv, w_o):
    B, T, _ = x.shape
    pos = jnp.arange(T, dtype=jnp.float32)
    inv = ROPE_THETA ** (-jnp.arange(0, MLA_ROPE, 2, dtype=jnp.float32) / MLA_ROPE)
    ang = pos[:, None] * inv[None, :]
    cos, sin = jnp.cos(ang), jnp.sin(ang)
    c_q = rms_norm(x @ w_dq, g_q)
    q = (c_q @ w_uq).reshape(B, T, MLA_HEADS, MLA_NOPE + MLA_ROPE)
    q_nope = q[..., :MLA_NOPE]
    q_rope = apply_rope(q[..., MLA_NOPE:], cos[:, None, :], sin[:, None, :])
    kv_a = x @ w_dkv
    c_kv = rms_norm(kv_a[..., :MLA_KV_RANK], g_kv)
    k_rope = apply_rope(kv_a[..., MLA_KV_RANK:], cos, sin)
    kv = (c_kv @ w_ukv).reshape(B, T, MLA_HEADS, MLA_NOPE + MLA_V)
    k_nope, v = kv[..., :MLA_NOPE], kv[..., MLA_NOPE:]
    scale = (MLA_NOPE + MLA_ROPE) ** -0.5
    nqb = T // Q_BLOCK
    qn_b = jnp.moveaxis(q_nope.reshape(B, nqb, Q_BLOCK, MLA_HEADS, MLA_NOPE), 1, 0)
    qr_b = jnp.moveaxis(q_rope.reshape(B, nqb, Q_BLOCK, MLA_HEADS, MLA_ROPE), 1, 0)

    def q_block(args):
        qn, qr = args
        s = (jnp.einsum('bqhd,bkhd->bhqk', qn, k_nope)
             + jnp.einsum('bqhd,bkd->bhqk', qr, k_rope)).astype(jnp.float32) * scale
        p = jax.nn.softmax(s, axis=-1).astype(x.dtype)
        return jnp.einsum('bhqk,bkhd->bqhd', p, v)

    o = lax.map(q_block, (qn_b, qr_b))
    o = jnp.moveaxis(o, 0, 1).reshape(B, T, MLA_HEADS * MLA_V)
    return o @ w_o


def setup_inputs(seed: int = 0) -> dict:
    key = jax.random.key(seed)
    ks = iter(jax.random.split(key, 40))
    n_a = len(range(0, DEPTH, N_MIXERS))
    n_b = len(range(1, DEPTH, N_MIXERS))
    n_c = len(range(2, DEPTH, N_MIXERS))
    C = LRU_WIDTH

    def dense(shape, fan_in):
        return jax.random.normal(next(ks), shape, jnp.float32) * fan_in ** -0.5

    def gain(shape):
        return 1.0 + 0.05 * jax.random.normal(next(ks), shape, jnp.float32)

    def small(shape, s):
        return s * jax.random.normal(next(ks), shape, jnp.float32)

    x_prompt = jax.random.normal(next(ks), (BATCH, SEQ, D_MODEL), jnp.float32)
    x_sample = jax.random.normal(next(ks), (DEC_BATCH, DEC_SEQ, D_MODEL), jnp.float32)
    a_c = jax.random.uniform(next(ks), (n_b, 2, C), jnp.float32, 0.9, 0.999)
    s = a_c ** (1.0 / LRU_C)
    lru_lam = jnp.log(s) - jnp.log1p(-s)
    return {
        "x_prompt": x_prompt,
        "x_sample": x_sample,
        "norm_mix": gain((DEPTH, D_MODEL)),
        "norm_ffn": gain((DEPTH, D_MODEL)),
        "norm_final": gain((D_MODEL,)),
        "na_w_qkv": dense((n_a, D_MODEL, 3 * D_MODEL), D_MODEL),
        "na_rpb": small((n_a, NA_HEADS, 2 * NA_KH_MAX - 1, 2 * NA_KW - 1), 0.2),
        "na_w_o": dense((n_a, D_MODEL, D_MODEL), D_MODEL),
        "lru_w_in": dense((n_b, D_MODEL, 2 * C), D_MODEL),
        "lru_conv_w": dense((n_b, CONV_W, C), CONV_W),
        "lru_conv_b": small((n_b, C), 0.02),
        "lru_w_a": dense((n_b, 2, LRU_BLOCKS, LRU_BLOCK, LRU_BLOCK), LRU_BLOCK),
        "lru_b_a": small((n_b, 2, C), 0.1),
        "lru_w_x": dense((n_b, 2, LRU_BLOCKS, LRU_BLOCK, LRU_BLOCK), LRU_BLOCK),
        "lru_b_x": small((n_b, 2, C), 0.1),
        "lru_lam": lru_lam,
        "lru_w_out": dense((n_b, C, D_MODEL), C),
        "mla_w_dq": dense((n_c, D_MODEL, MLA_Q_RANK), D_MODEL),
        "mla_g_q": gain((n_c, MLA_Q_RANK)),
        "mla_w_uq": dense((n_c, MLA_Q_RANK, MLA_HEADS * (MLA_NOPE + MLA_ROPE)), MLA_Q_RANK),
        "mla_w_dkv": dense((n_c, D_MODEL, MLA_KV_RANK + MLA_ROPE), D_MODEL),
        "mla_g_kv": gain((n_c, MLA_KV_RANK)),
        "mla_w_ukv": dense((n_c, MLA_KV_RANK, MLA_HEADS * (MLA_NOPE + MLA_V)), MLA_KV_RANK),
        "mla_w_o": dense((n_c, MLA_HEADS * MLA_V, D_MODEL), MLA_HEADS * MLA_V),
        "ffn_w_gate": dense((DEPTH, D_MODEL, D_FF), D_MODEL),
        "ffn_w_up": dense((DEPTH, D_MODEL, D_FF), D_MODEL),
        "ffn_w_down": dense((DEPTH, D_FF, D_MODEL), D_FF),
    }


def reference(x_prompt, x_sample, norm_mix, norm_ffn, norm_final,
              na_w_qkv, na_rpb, na_w_o,
              lru_w_in, lru_conv_w, lru_conv_b, lru_w_a, lru_b_a, lru_w_x, lru_b_x, lru_lam, lru_w_out,
              mla_w_dq, mla_g_q, mla_w_uq, mla_w_dkv, mla_g_kv, mla_w_ukv, mla_w_o,
              ffn_w_gate, ffn_w_up, ffn_w_down):
    def run(x):
        for i in range(DEPTH):
            j = i // N_MIXERS
            m = i % N_MIXERS
            h = rms_norm(x, norm_mix[i])
            if m == 0:
                y = neighbourhood_attention(h, na_w_qkv[j], na_rpb[j], na_w_o[j])
            elif m == 1:
                y = recurrent_block(h, lru_w_in[j], lru_conv_w[j], lru_conv_b[j], lru_w_a[j], lru_b_a[j],
                                    lru_w_x[j], lru_b_x[j], lru_lam[j], lru_w_out[j])
            else:
                y = mla(h, mla_w_dq[j], mla_g_q[j], mla_w_uq[j], mla_w_dkv[j], mla_g_kv[j],
                        mla_w_ukv[j], mla_w_o[j])
            x = x + y
            x = x + swiglu(rms_norm(x, norm_ffn[i]), ffn_w_gate[i], ffn_w_up[i], ffn_w_down[i])
        return rms_norm(x, norm_final)

    y_prompt = run(x_prompt)
    y_sample = run(x_sample)
    return (y_prompt, y_sample)
```

```python
import functools

import numpy as np
import jax
import jax.numpy as jnp
from jax import lax
from jax.experimental import pallas as pl
from jax.experimental.pallas import tpu as pltpu

F32 = jnp.float32
BF16 = jnp.bfloat16

D_MODEL = 1024
DEPTH = 4
N_MIXERS = 3
RMS_EPS = 1e-6
D_FF = 2816

GRID_W = 64
NA_HEADS = 16
NA_HEAD_DIM = 64
NA_KH = 8
NA_KW = 16
NA_QBW = 16
NA_KBW = 32
NA_NCB = GRID_W // NA_QBW
NA_BAND_START = tuple(int(v) for v in np.clip(np.arange(NA_NCB) * NA_QBW - NA_KW // 2, 0, GRID_W - NA_KBW))
NA_RB = 8
NA_KROWS = 16
NA_KSUB = 4
NA_NE = 28
NEG = -1e30

LRU_WIDTH = 1408
LRU_BLOCKS = 16
LRU_BLOCK = 88
CONV_W = 4
LRU_C = 8.0
HALO = 8

MLA_HEADS = 16
MLA_Q_RANK = 384
MLA_KV_RANK = 256
MLA_NOPE = 64
MLA_ROPE = 32
MLA_V = 64
ROPE_THETA = 10000.0

LANE = 128
TOK_TILE = 512
FF_CHUNK = 256
MLA_TQ = 256
MLA_TK = 512
VMEM_LIMIT = 56 * 1024 * 1024


def _cparams(*sem):
    return pltpu.CompilerParams(dimension_semantics=sem, vmem_limit_bytes=VMEM_LIMIT)


def _resident(shape):
    nd = len(shape)
    return pl.BlockSpec(shape, lambda *_: (0,) * nd, pipeline_mode=pl.Buffered(1))


def _rms(x, g):
    return x * lax.rsqrt(jnp.mean(x * x, axis=-1, keepdims=True) + RMS_EPS) * g


def _dot(a, b):
    return jnp.dot(a, b, preferred_element_type=F32)


def _dot_nt(a, b):
    return lax.dot_general(a, b, (((1,), (1,)), ((), ())), preferred_element_type=F32)


def _mm_res_kernel(a_ref, w_ref, r_ref, o_ref):
    o_ref[...] = r_ref[...] + _dot(a_ref[...], w_ref[...])


def _mm_res(a, w, res):
    m, k = a.shape
    n = w.shape[1]
    return pl.pallas_call(
        _mm_res_kernel,
        out_shape=jax.ShapeDtypeStruct((m, n), F32),
        grid=(m // TOK_TILE,),
        in_specs=[pl.BlockSpec((TOK_TILE, k), lambda i: (i, 0)),
                  _resident((k, n)),
                  pl.BlockSpec((TOK_TILE, n), lambda i: (i, 0))],
        out_specs=pl.BlockSpec((TOK_TILE, n), lambda i: (i, 0)),
        compiler_params=_cparams("parallel"),
        name="mm_res",
    )(a, w, res)


def _ffn_kernel(x_ref, g_ref, wg_ref, wu_ref, wd_ref, *rest, final):
    o_ref = rest[-1]
    x = x_ref[...]
    xn = _rms(x, g_ref[...]).astype(BF16)
    acc = x
    for c in range(D_FF // FF_CHUNK):
        sl = slice(c * FF_CHUNK, (c + 1) * FF_CHUNK)
        g = _dot(xn, wg_ref[:, sl])
        u = _dot(xn, wu_ref[:, sl])
        h = (g * jax.nn.sigmoid(g) * u).astype(BF16)
        acc = acc + _dot(h, wd_ref[sl, :])
    if final:
        acc = _rms(acc, rest[0][...])
    o_ref[...] = acc


def _ffn(x, g, wg, wu, wd, g_final=None):
    m = x.shape[0]
    final = g_final is not None
    ins = [x, g.reshape(1, D_MODEL), wg, wu, wd]
    specs = [pl.BlockSpec((TOK_TILE, D_MODEL), lambda i: (i, 0)),
             _resident((1, D_MODEL)),
             _resident((D_MODEL, D_FF)), _resident((D_MODEL, D_FF)), _resident((D_FF, D_MODEL))]
    if final:
        ins.append(g_final.reshape(1, D_MODEL))
        specs.append(_resident((1, D_MODEL)))
    return pl.pallas_call(
        functools.partial(_ffn_kernel, final=final),
        out_shape=jax.ShapeDtypeStruct((m, D_MODEL), F32),
        grid=(m // TOK_TILE,),
        in_specs=specs,
        out_specs=pl.BlockSpec((TOK_TILE, D_MODEL), lambda i: (i, 0)),
        compiler_params=_cparams("parallel"),
        name="ffn",
    )(*ins)


def _na_qkv_kernel(x_ref, g_ref, w_ref, q_ref, kb_ref, vb_ref):
    xn = _rms(x_ref[0], g_ref[...]).astype(BF16)
    q = _dot(xn, w_ref[:, 0:D_MODEL])
    q_ref[0] = (q * (NA_HEAD_DIM ** -0.5)).astype(BF16)
    for part, out_ref in ((1, kb_ref), (2, vb_ref)):
        r = _dot(xn, w_ref[:, part * D_MODEL:(part + 1) * D_MODEL])
        r3 = r.reshape(NA_RB, GRID_W, D_MODEL)
        for j, bs in enumerate(NA_BAND_START):
            out_ref[0, :, j] = r3[:, bs:bs + NA_KBW, :].astype(BF16)


def _na_qkv(x, g, w):
    b, t, _ = x.shape
    rows = t // GRID_W
    tile = NA_RB * GRID_W
    band = jax.ShapeDtypeStruct((b, rows, NA_NCB, NA_KBW, D_MODEL), BF16)
    band_spec = pl.BlockSpec((1, NA_RB, NA_NCB, NA_KBW, D_MODEL), lambda i, r: (i, r, 0, 0, 0))
    return pl.pallas_call(
        _na_qkv_kernel,
        out_shape=(jax.ShapeDtypeStruct((b, t, D_MODEL), BF16), band, band),
        grid=(b, rows // NA_RB),
        in_specs=[pl.BlockSpec((1, tile, D_MODEL), lambda i, r: (i, r, 0)),
                  _resident((1, D_MODEL)),
                  _resident((D_MODEL, 3 * D_MODEL))],
        out_specs=(pl.BlockSpec((1, tile, D_MODEL), lambda i, r: (i, r, 0)), band_spec, band_spec),
        compiler_params=_cparams("parallel", "parallel"),
        name="na_qkv",
    )(x, g.reshape(1, D_MODEL), w)


def _na_key_block0(rb, nkb):
    return jnp.clip(2 * rb - 1, 0, nkb - NA_KROWS // NA_KSUB)


def _na_attn_kernel(q_ref, k0, k1, k2, k3, v0, v1, v2, v3, bias_ref, rmask_ref, o_ref, *, nkb):
    rb = pl.program_id(2)
    var = 2 * rb - _na_key_block0(rb, nkb)
    lo = lax.broadcasted_iota(jnp.int32, (1, LANE), 1) < NA_HEAD_DIM
    nq = NA_RB * NA_QBW
    for hp in range(NA_HEADS // 2):
        sl = slice(hp * LANE, (hp + 1) * LANE)
        qp = q_ref[0, :, :, sl].reshape(nq, LANE)
        zero = jnp.zeros_like(qp)
        ql = jnp.concatenate([jnp.where(lo, qp, zero), jnp.where(lo, zero, qp)], axis=0)
        kp = jnp.concatenate([r[0, :, 0, :, sl].reshape(NA_KSUB * NA_KBW, LANE) for r in (k0, k1, k2, k3)], axis=0)
        vp = jnp.concatenate([r[0, :, 0, :, sl].reshape(NA_KSUB * NA_KBW, LANE) for r in (v0, v1, v2, v3)], axis=0)
        s = _dot_nt(ql, kp)
        ps, ls = [], []
        for a in range(2):
            for qr in range(NA_RB):
                r0 = a * nq + qr * NA_QBW
                sc = s[r0:r0 + NA_QBW, :]
                bias = jnp.concatenate(
                    [bias_ref[0, 2 * hp + a, 4 * g - qr + 15 - 4 * var] + rmask_ref[var, qr, g]
                     for g in range(NA_KROWS // NA_KSUB)], axis=1)
                sc = sc + bias
                p = jnp.exp(sc - jnp.max(sc, axis=1, keepdims=True))
                ls.append(jnp.sum(p, axis=1, keepdims=True))
                ps.append(p.astype(BF16))
        o = _dot(jnp.concatenate(ps, axis=0), vp) / jnp.concatenate(ls, axis=0)
        out = jnp.where(lo, o[:nq], o[nq:]).astype(BF16)
        o_ref[0, :, :, sl] = out.reshape(NA_RB, NA_QBW, LANE)


def _na_tables(rpb):
    j = np.arange(NA_NCB)[:, None, None, None]
    e = np.arange(NA_NE)[None, :, None, None]
    qc = np.arange(NA_QBW)[None, None, :, None]
    ln = np.arange(LANE)[None, None, None, :]
    dr = (e - 8) + ln // NA_KBW
    kcol = np.asarray(NA_BAND_START)[j] + ln % NA_KBW
    c = NA_QBW * j + qc
    ws = np.clip(c - NA_KW // 2, 0, GRID_W - NA_KW)
    col_ok = (kcol >= ws) & (kcol < ws + NA_KW)
    dr_ok = (dr >= 0) & (dr <= 2 * NA_KH - 2)
    dc = np.clip(kcol - c + NA_KW - 1, 0, 2 * NA_KW - 2)
    shape = (NA_NCB, NA_NE, NA_QBW, LANE)
    dr_i = np.broadcast_to(np.clip(dr, 0, 2 * NA_KH - 2), shape)
    dc_i = np.broadcast_to(dc, shape)
    tab = rpb.astype(F32)[:, dr_i, dc_i]
    tab = jnp.where(np.broadcast_to(dr_ok, shape), tab, 0.0)
    tab = jnp.where(np.broadcast_to(col_ok, shape), tab, NEG)
    tab = jnp.transpose(tab, (1, 0, 2, 3, 4))

    var = np.arange(3)[:, None, None, None]
    qr = np.arange(NA_RB)[None, :, None, None]
    g = np.arange(NA_KROWS // NA_KSUB)[None, None, :, None]
    kr = NA_KSUB * g + np.arange(LANE)[None, None, None, :] // NA_KBW
    rs = np.where(var == 0, np.maximum(qr - NA_KH // 2, 0),
                  np.where(var == 1, qr, np.minimum(qr + NA_KH // 2, NA_KH)))
    row_ok = (kr >= rs) & (kr < rs + NA_KH)
    rmask = np.where(row_ok, 0.0, NEG).astype(np.float32)
    rmask = np.broadcast_to(rmask[:, :, :, None, :], (3, NA_RB, NA_KROWS // NA_KSUB, NA_QBW, LANE))
    return tab, jnp.asarray(np.ascontiguousarray(rmask))


def _na_attn(q, kb, vb, rpb):
    b, t, _ = q.shape
    rows = t // GRID_W
    assert rows % NA_RB == 0 and rows >= NA_KROWS
    nkb = rows // NA_KSUB
    q4 = q.reshape(b, rows, GRID_W, D_MODEL)
    tab, rmask = _na_tables(rpb)
    q_spec = pl.BlockSpec((1, NA_RB, NA_QBW, D_MODEL), lambda j, i, r: (i, r, j, 0))

    def band_spec(off):
        return pl.BlockSpec((1, NA_KSUB, 1, NA_KBW, D_MODEL),
                            lambda j, i, r: (i, _na_key_block0(r, nkb) + off, j, 0, 0))

    o = pl.pallas_call(
        functools.partial(_na_attn_kernel, nkb=nkb),
        out_shape=jax.ShapeDtypeStruct((b, rows, GRID_W, D_MODEL), BF16),
        grid=(NA_NCB, b, rows // NA_RB),
        in_specs=[q_spec] + [band_spec(off) for off in range(4)] * 2 + [
            pl.BlockSpec((1, NA_HEADS, NA_NE, NA_QBW, LANE), lambda j, i, r: (j, 0, 0, 0, 0)),
            _resident(rmask.shape)],
        out_specs=q_spec,
        compiler_params=_cparams("parallel", "parallel", "parallel"),
        name="na_attn",
    )(q4, kb, kb, kb, kb, vb, vb, vb, vb, tab, rmask)
    return o.reshape(b, t, D_MODEL)


def _na_layer(x, g, w_qkv, rpb, w_o):
    b, t, _ = x.shape
    q, kb, vb = _na_qkv(x, g, w_qkv.astype(BF16))
    o = _na_attn(q, kb, vb, rpb)
    return _mm_res(o.reshape(b * t, D_MODEL), w_o.astype(BF16), x.reshape(b * t, D_MODEL))


def _lru_in_kernel(x_ref, g_ref, w_ref, gate_ref, xr_ref):
    xn = _rms(x_ref[...], g_ref[...]).astype(BF16)
    gate_ref[...] = _dot(xn, w_ref[:, 0:LRU_WIDTH])
    xr_ref[...] = _dot(xn, w_ref[:, LRU_WIDTH:2 * LRU_WIDTH])


def _lru_in(x, g, w):
    m = x.shape[0]
    out = jax.ShapeDtypeStruct((m, LRU_WIDTH), F32)
    spec = pl.BlockSpec((TOK_TILE, LRU_WIDTH), lambda i: (i, 0))
    return pl.pallas_call(
        _lru_in_kernel,
        out_shape=(out, out),
        grid=(m // TOK_TILE,),
        in_specs=[pl.BlockSpec((TOK_TILE, D_MODEL), lambda i: (i, 0)),
                  _resident((1, D_MODEL)),
                  _resident((D_MODEL, 2 * LRU_WIDTH))],
        out_specs=(spec, spec),
        compiler_params=_cparams("parallel"),
        name="lru_in",
    )(x, g.reshape(1, D_MODEL), w)


def _lru_scan_tile(xr_ref, prev_ref, next_ref, cw_ref, vec_ref, wg_ref,
                   ext_ref, a_ref, u_ref, h_ref, carry_ref, *, tile, n_tiles, reverse):
    step = pl.program_id(1)
    ti = (n_tiles - 1 - step) if reverse else step
    tt = tile

    @pl.when(step == 0)
    def _():
        carry_ref[...] = jnp.zeros_like(carry_ref)

    ext_ref[0:HALO, :] = jnp.where(ti > 0, prev_ref[0], 0.0)
    ext_ref[HALO:HALO + tt, :] = xr_ref[0]
    ext_ref[HALO + tt:2 * HALO + tt, :] = jnp.where(ti < n_tiles - 1, next_ref[0], 0.0)
    xc = vec_ref[0:1, :]
    for k in range(CONV_W):
        xc = xc + cw_ref[k:k + 1, :] * ext_ref[pl.ds(HALO - 2 + k, tt), :]
    xcb = xc.astype(BF16)
    r = jax.nn.sigmoid(_dot(xcb, wg_ref[:, 0:LRU_WIDTH]) + vec_ref[1:2, :])
    i = jax.nn.sigmoid(_dot(xcb, wg_ref[:, LRU_WIDTH:2 * LRU_WIDTH]) + vec_ref[2:3, :])
    z = -vec_ref[3:4, :]
    softplus = jnp.maximum(z, 0.0) + jnp.log1p(jnp.exp(-jnp.abs(z)))
    log_a = (-LRU_C * r) * softplus
    a = jnp.exp(log_a)
    u = jnp.sqrt(-jnp.tanh(log_a) * (a * a + 1.0)) * (i * xc)
    a_ref[...] = a
    u_ref[...] = u

    def body(k, h):
        t = (tt - 1 - k) if reverse else k
        h = a_ref[pl.ds(t, 1), :] * h + u_ref[pl.ds(t, 1), :]
        h_ref[pl.ds(t, 1), :] = h
        return h

    carry_ref[...] = lax.fori_loop(0, tt, body, carry_ref[...], unroll=8)


def _lru_bwd_kernel(xr_ref, prev_ref, next_ref, cw_ref, vec_ref, wg_ref, hb_ref,
                    ext_ref, a_ref, u_ref, h_ref, carry_ref, *, tile, n_tiles):
    _lru_scan_tile(xr_ref, prev_ref, next_ref, cw_ref, vec_ref, wg_ref,
                   ext_ref, a_ref, u_ref, h_ref, carry_ref, tile=tile, n_tiles=n_tiles, reverse=True)
    hb_ref[0] = h_ref[...]


def _lru_fwd_kernel(xr_ref, prev_ref, next_ref, cw_ref, vec_ref, wg_ref, hb_ref, gate_ref, res_ref, wo_ref,
                    o_ref, ext_ref, a_ref, u_ref, h_ref, carry_ref, *, tile, n_tiles):
    _lru_scan_tile(xr_ref, prev_ref, next_ref, cw_ref, vec_ref, wg_ref,
                   ext_ref, a_ref, u_ref, h_ref, carry_ref, tile=tile, n_tiles=n_tiles, reverse=False)
    y = jax.nn.gelu(gate_ref[0], approximate=True) * (h_ref[...] + hb_ref[0])
    o_ref[0] = res_ref[0] + _dot(y.astype(BF16), wo_ref[...])


def _block_diag(w):
    eye = jnp.eye(LRU_BLOCKS, dtype=w.dtype)
    return jnp.einsum('ncd,nm->ncmd', w, eye).reshape(LRU_WIDTH, LRU_WIDTH)


def _lru_layer(x, g, w_in, conv_w, conv_b, w_a, b_a, w_x, b_x, lam, w_out):
    b, t, _ = x.shape
    c = LRU_WIDTH
    tile = TOK_TILE
    n_tiles = t // tile
    hpt = tile // HALO
    gate, xr = _lru_in(x.reshape(b * t, D_MODEL), g, w_in.astype(BF16))
    gate = gate.reshape(b, t, c)
    xr = xr.reshape(b, t, c)

    def tile_idx(s, reverse):
        return (n_tiles - 1 - s) if reverse else s

    def scan_specs(reverse):
        return [
            pl.BlockSpec((1, tile, c), lambda i, s: (i, tile_idx(s, reverse), 0)),
            pl.BlockSpec((1, HALO, c), lambda i, s: (i, jnp.maximum(tile_idx(s, reverse) * hpt - 1, 0), 0)),
            pl.BlockSpec((1, HALO, c),
                         lambda i, s: (i, jnp.minimum((tile_idx(s, reverse) + 1) * hpt, t // HALO - 1), 0)),
            _resident((CONV_W, c)),
            _resident((4, c)),
            _resident((c, 2 * c)),
        ]

    scratch = [pltpu.VMEM((tile + 2 * HALO, c), F32), pltpu.VMEM((tile, c), F32), pltpu.VMEM((tile, c), F32),
               pltpu.VMEM((tile, c), F32), pltpu.VMEM((1, c), F32)]

    def direction_params(d):
        vec = jnp.stack([conv_b, b_a[d], b_x[d], lam[d]]).astype(F32)
        wg = jnp.concatenate([_block_diag(w_a[d]), _block_diag(w_x[d])], axis=1).astype(BF16)
        return vec, wg

    vec1, wg1 = direction_params(1)
    hb = pl.pallas_call(
        functools.partial(_lru_bwd_kernel, tile=tile, n_tiles=n_tiles),
        out_shape=jax.ShapeDtypeStruct((b, t, c), F32),
        grid=(b, n_tiles),
        in_specs=scan_specs(True),
        out_specs=pl.BlockSpec((1, tile, c), lambda i, s: (i, tile_idx(s, True), 0)),
        scratch_shapes=scratch,
        compiler_params=_cparams("arbitrary", "arbitrary"),
        name="lru_bwd",
    )(xr, xr, xr, conv_w, vec1, wg1)

    vec0, wg0 = direction_params(0)
    tok = lambda n: pl.BlockSpec((1, tile, n), lambda i, s: (i, s, 0))
    return pl.pallas_call(
        functools.partial(_lru_fwd_kernel, tile=tile, n_tiles=n_tiles),
        out_shape=jax.ShapeDtypeStruct((b, t, D_MODEL), F32),
        grid=(b, n_tiles),
        in_specs=scan_specs(False) + [tok(c), tok(c), tok(D_MODEL), _resident((c, D_MODEL))],
        out_specs=tok(D_MODEL),
        scratch_shapes=scratch,
        compiler_params=_cparams("arbitrary", "arbitrary"),
        name="lru_fwd",
    )(xr, xr, xr, conv_w, vec0, wg0, hb, gate, x, w_out.astype(BF16)).reshape(b * t, D_MODEL)


def _rope_lanes(v, cos, sin):
    return v * cos + pltpu.roll(v, LANE // 2, 1) * sin


def _mla_proj_kernel(x_ref, g_ref, wdq_ref, gq_ref, wuq_ref, wdkv_ref, gkv_ref, wukv_ref, cos_ref, sin_ref,
                     qn_ref, qr_ref, kn_ref, kr_ref, v_ref):
    scale = (MLA_NOPE + MLA_ROPE) ** -0.5
    nq = MLA_HEADS * MLA_NOPE
    xn = _rms(x_ref[0], g_ref[...]).astype(BF16)
    cos = cos_ref[...]
    sin = sin_ref[...]
    cq = _rms(_dot(xn, wdq_ref[...]), gq_ref[...]).astype(BF16)
    q = _dot(cq, wuq_ref[...])
    qn_ref[0] = (q[:, 0:nq] * scale).astype(BF16)
    for p in range(MLA_HEADS // 2):
        sl = slice(nq + p * LANE, nq + (p + 1) * LANE)
        qr_ref[0, :, p * LANE:(p + 1) * LANE] = (_rope_lanes(q[:, sl], cos, sin) * scale).astype(BF16)
    kva = _dot(xn, wdkv_ref[...])
    kr_ref[0] = _rope_lanes(kva[:, MLA_KV_RANK:MLA_KV_RANK + LANE], cos, sin).astype(BF16)
    ckv = _rms(kva[:, 0:MLA_KV_RANK], gkv_ref[...]).astype(BF16)
    kv = _dot(ckv, wukv_ref[...])
    kn_ref[0] = kv[:, 0:nq].astype(BF16)
    v_ref[0] = kv[:, nq:2 * nq].astype(BF16)


def _mla_attn_kernel(qn_ref, qr_ref, kn_ref, kr_ref, v_ref, o_ref, *, seq):
    tq, tk = MLA_TQ, MLA_TK
    qp = jnp.concatenate([qn_ref[0], qr_ref[0]], axis=1)
    lane = lax.broadcasted_iota(jnp.int32, (1, 2 * LANE), 1)
    in_a = (lane < MLA_NOPE) | ((lane >= LANE) & (lane < LANE + MLA_ROPE))
    in_b = ((lane >= MLA_NOPE) & (lane < LANE)) | ((lane >= LANE + MLA_ROPE) & (lane < LANE + 2 * MLA_ROPE))
    zero = jnp.zeros_like(qp)
    ql = jnp.concatenate([jnp.where(in_a, qp, zero), jnp.where(in_b, qp, zero)], axis=0)

    def body(kt, carry):
        m, l, acc = carry
        ks = pl.multiple_of(kt * tk, tk)
        kk = jnp.concatenate([kn_ref[0, pl.ds(ks, tk), :], kr_ref[0, pl.ds(ks, tk), :]], axis=1)
        s = _dot_nt(ql, kk)
        m_new = jnp.maximum(m, jnp.max(s, axis=1, keepdims=True))
        alpha = jnp.exp(m - m_new)
        p = jnp.exp(s - m_new)
        l = alpha * l + jnp.sum(p, axis=1, keepdims=True)
        acc = alpha * acc + _dot(p.astype(BF16), v_ref[0, pl.ds(ks, tk), :])
        return m_new, l, acc

    init = (jnp.full((2 * tq, 1), NEG, F32), jnp.zeros((2 * tq, 1), F32), jnp.zeros((2 * tq, LANE), F32))
    _, l, acc = lax.fori_loop(0, seq // tk, body, init)
    o = acc / l
    lo = lax.broadcasted_iota(jnp.int32, (1, LANE), 1) < MLA_V
    o_ref[0] = jnp.where(lo, o[:tq], o[tq:]).astype(BF16)


def _mla_weights(w_uq, w_dkv, w_ukv):
    h, nope, rope = MLA_HEADS, MLA_NOPE, MLA_ROPE
    half = rope // 2
    wq = w_uq.reshape(MLA_Q_RANK, h, nope + rope)
    wq_nope = wq[:, :, :nope].reshape(MLA_Q_RANK, h * nope)
    wq_rope = wq[:, :, nope:].reshape(MLA_Q_RANK, h // 2, 2, rope)
    wq_swap = jnp.concatenate([wq_rope[..., half:], wq_rope[..., :half]], axis=-1)
    wq_rope = jnp.concatenate([wq_rope.reshape(MLA_Q_RANK, h // 2, 2 * rope),
                               wq_swap.reshape(MLA_Q_RANK, h // 2, 2 * rope)], axis=-1)
    wuq = jnp.concatenate([wq_nope, wq_rope.reshape(MLA_Q_RANK, h // 2 * LANE)], axis=1)
    w_kr = w_dkv[:, MLA_KV_RANK:]
    w_krs = jnp.concatenate([w_kr[:, half:], w_kr[:, :half]], axis=1)
    wdkv = jnp.concatenate([w_dkv[:, :MLA_KV_RANK], w_kr, w_kr, w_krs, w_krs], axis=1)
    wkv = w_ukv.reshape(MLA_KV_RANK, h, nope + MLA_V)
    wukv = jnp.concatenate([wkv[:, :, :nope].reshape(MLA_KV_RANK, h * nope),
                            wkv[:, :, nope:].reshape(MLA_KV_RANK, h * MLA_V)], axis=1)
    return wuq.astype(BF16), wdkv.astype(BF16), wukv.astype(BF16)


def _rope_tables(t):
    pos = jnp.arange(t, dtype=F32)
    inv = ROPE_THETA ** (-jnp.arange(0, MLA_ROPE, 2, dtype=F32) / MLA_ROPE)
    ang = pos[:, None] * inv[None, :]
    cos, sin = jnp.cos(ang), jnp.sin(ang)
    pad = jnp.zeros((t, LANE // 2), F32)
    return (jnp.concatenate([cos, cos, cos, cos, pad], axis=1),
            jnp.concatenate([-sin, sin, -sin, sin, pad], axis=1))


def _mla_layer(x, g, w_dq, g_q, w_uq, w_dkv, g_kv, w_ukv, w_o):
    b, t, _ = x.shape
    tm = TOK_TILE
    wuq, wdkv, wukv = _mla_weights(w_uq, w_dkv, w_ukv)
    cos, sin = _rope_tables(t)
    wide = MLA_HEADS * MLA_NOPE
    tok = lambda n: pl.BlockSpec((1, tm, n), lambda i, s: (i, s, 0))
    big = jax.ShapeDtypeStruct((b, t, wide), BF16)
    qn, qr, kn, kr, v = pl.pallas_call(
        _mla_proj_kernel,
        out_shape=(big, big, big, jax.ShapeDtypeStruct((b, t, LANE), BF16), big),
        grid=(b, t // tm),
        in_specs=[tok(D_MODEL), _resident((1, D_MODEL)),
                  _resident((D_MODEL, MLA_Q_RANK)), _resident((1, MLA_Q_RANK)), _resident(wuq.shape),
                  _resident(wdkv.shape), _resident((1, MLA_KV_RANK)), _resident(wukv.shape),
                  pl.BlockSpec((tm, LANE), lambda i, s: (s, 0)), pl.BlockSpec((tm, LANE), lambda i, s: (s, 0))],
        out_specs=(tok(wide), tok(wide), tok(wide), tok(LANE), tok(wide)),
        compiler_params=_cparams("parallel", "parallel"),
        name="mla_proj",
    )(x, g.reshape(1, D_MODEL), w_dq.astype(BF16), g_q.reshape(1, MLA_Q_RANK), wuq,
      wdkv, g_kv.reshape(1, MLA_KV_RANK), wukv, cos, sin)

    q_spec = pl.BlockSpec((1, MLA_TQ, LANE), lambda i, p, s: (i, s, p))
    kv_spec = pl.BlockSpec((1, t, LANE), lambda i, p, s: (i, 0, p))
    o = pl.pallas_call(
        functools.partial(_mla_attn_kernel, seq=t),
        out_shape=jax.ShapeDtypeStruct((b, t, MLA_HEADS * MLA_V), BF16),
        grid=(b, MLA_HEADS // 2, t // MLA_TQ),
        in_specs=[q_spec, q_spec, kv_spec,
                  pl.BlockSpec((1, t, LANE), lambda i, p, s: (i, 0, 0)), kv_spec],
        out_specs=q_spec,
        compiler_params=_cparams("parallel", "parallel", "parallel"),
        name="mla_attn",
    )(qn, qr, kn, kr, v)
    return _mm_res(o.reshape(b * t, MLA_HEADS * MLA_V), w_o.astype(BF16), x.reshape(b * t, D_MODEL))


def kernel(x_prompt, x_sample, norm_mix, norm_ffn, norm_final, na_w_qkv, na_rpb, na_w_o, lru_w_in, lru_conv_w, lru_conv_b, lru_w_a, lru_b_a, lru_w_x, lru_b_x, lru_lam, lru_w_out, mla_w_dq, mla_g_q, mla_w_uq, mla_w_dkv, mla_g_kv, mla_w_ukv, mla_w_o, ffn_w_gate, ffn_w_up, ffn_w_down):
    wg = ffn_w_gate.astype(BF16)
    wu = ffn_w_up.astype(BF16)
    wd = ffn_w_down.astype(BF16)

    def run(x):
        b, t, _ = x.shape
        for i in range(DEPTH):
            j, m = divmod(i, N_MIXERS)
            x3 = x.reshape(b, t, D_MODEL)
            if m == 0:
                x = _na_layer(x3, norm_mix[i], na_w_qkv[j], na_rpb[j], na_w_o[j])
            elif m == 1:
                x = _lru_layer(x3, norm_mix[i], lru_w_in[j], lru_conv_w[j], lru_conv_b[j], lru_w_a[j], lru_b_a[j],
                               lru_w_x[j], lru_b_x[j], lru_lam[j], lru_w_out[j])
            else:
                x = _mla_layer(x3, norm_mix[i], mla_w_dq[j], mla_g_q[j], mla_w_uq[j], mla_w_dkv[j], mla_g_kv[j],
                               mla_w_ukv[j], mla_w_o[j])
            x = _ffn(x, norm_ffn[i], wg[i], wu[i], wd[i], norm_final if i == DEPTH - 1 else None)
        return x.reshape(b, t, D_MODEL)

    return run(x_prompt), run(x_sample)
```

```python
import functools

import numpy as np
import jax
import jax.numpy as jnp
from jax import lax
from jax.experimental import pallas as pl
from jax.experimental.pallas import tpu as pltpu

F32 = jnp.float32
BF16 = jnp.bfloat16

D_MODEL = 1024
DEPTH = 4
N_MIXERS = 3
RMS_EPS = 1e-6
D_FF = 2816

GRID_W = 64
NA_HEADS = 16
NA_HEAD_DIM = 64
NA_KH = 8
NA_KW = 16
NA_QBW = 16
NA_KBW = 32
NA_NCB = GRID_W // NA_QBW
NA_BAND_START = tuple(int(v) for v in np.clip(np.arange(NA_NCB) * NA_QBW - NA_KW // 2, 0, GRID_W - NA_KBW))
NA_RB = 8
NA_KROWS = 16
NA_KSUB = 4
NA_NE = 28
NEG = -1e30

LRU_WIDTH = 1408
LRU_BLOCKS = 16
LRU_BLOCK = 88
CONV_W = 4
LRU_C = 8.0
HALO = 8

MLA_HEADS = 16
MLA_Q_RANK = 384
MLA_KV_RANK = 256
MLA_NOPE = 64
MLA_ROPE = 32
MLA_V = 64
ROPE_THETA = 10000.0

LANE = 128
TOK_TILE = 512
FF_CHUNK = 256
MLA_TQ = 256
MLA_TK = 512
MLA_CH = 32
VMEM_LIMIT = 56 * 1024 * 1024


def _cparams(*sem):
    return pltpu.CompilerParams(dimension_semantics=sem, vmem_limit_bytes=VMEM_LIMIT)


def _resident(shape):
    nd = len(shape)
    return pl.BlockSpec(shape, lambda *_: (0,) * nd, pipeline_mode=pl.Buffered(1))


def _rms(x, g):
    return x * lax.rsqrt(jnp.mean(x * x, axis=-1, keepdims=True) + RMS_EPS) * g


def _dot(a, b):
    return jnp.dot(a, b, preferred_element_type=F32)


def _dot_nt(a, b):
    return lax.dot_general(a, b, (((1,), (1,)), ((), ())), preferred_element_type=F32)


def _mm_res_kernel(a_ref, w_ref, r_ref, o_ref):
    o_ref[...] = r_ref[...] + _dot(a_ref[...], w_ref[...])


def _mm_res(a, w, res):
    m, k = a.shape
    n = w.shape[1]
    return pl.pallas_call(
        _mm_res_kernel,
        out_shape=jax.ShapeDtypeStruct((m, n), F32),
        grid=(m // TOK_TILE,),
        in_specs=[pl.BlockSpec((TOK_TILE, k), lambda i: (i, 0)),
                  _resident((k, n)),
                  pl.BlockSpec((TOK_TILE, n), lambda i: (i, 0))],
        out_specs=pl.BlockSpec((TOK_TILE, n), lambda i: (i, 0)),
        compiler_params=_cparams("parallel"),
        name="mm_res",
    )(a, w, res)


def _ffn_kernel(x_ref, g_ref, wg_ref, wu_ref, wd_ref, *rest, final):
    o_ref = rest[-1]
    x = x_ref[...]
    xn = _rms(x, g_ref[...]).astype(BF16)
    acc = x
    for c in range(D_FF // FF_CHUNK):
        sl = slice(c * FF_CHUNK, (c + 1) * FF_CHUNK)
        g = _dot(xn, wg_ref[:, sl])
        u = _dot(xn, wu_ref[:, sl])
        h = (g * jax.nn.sigmoid(g) * u).astype(BF16)
        acc = acc + _dot(h, wd_ref[sl, :])
    if final:
        acc = _rms(acc, rest[0][...])
    o_ref[...] = acc


def _ffn(x, g, wg, wu, wd, g_final=None):
    m = x.shape[0]
    final = g_final is not None
    ins = [x, g.reshape(1, D_MODEL), wg, wu, wd]
    specs = [pl.BlockSpec((TOK_TILE, D_MODEL), lambda i: (i, 0)),
             _resident((1, D_MODEL)),
             _resident((D_MODEL, D_FF)), _resident((D_MODEL, D_FF)), _resident((D_FF, D_MODEL))]
    if final:
        ins.append(g_final.reshape(1, D_MODEL))
        specs.append(_resident((1, D_MODEL)))
    return pl.pallas_call(
        functools.partial(_ffn_kernel, final=final),
        out_shape=jax.ShapeDtypeStruct((m, D_MODEL), F32),
        grid=(m // TOK_TILE,),
        in_specs=specs,
        out_specs=pl.BlockSpec((TOK_TILE, D_MODEL), lambda i: (i, 0)),
        compiler_params=_cparams("parallel"),
        name="ffn",
    )(*ins)


def _na_qkv_kernel(x_ref, g_ref, w_ref, q_ref, kb_ref, vb_ref):
    xn = _rms(x_ref[0], g_ref[...]).astype(BF16)
    q = _dot(xn, w_ref[:, 0:D_MODEL])
    q_ref[0] = (q * (NA_HEAD_DIM ** -0.5)).astype(BF16)
    for part, out_ref in ((1, kb_ref), (2, vb_ref)):
        r = _dot(xn, w_ref[:, part * D_MODEL:(part + 1) * D_MODEL])
        r3 = r.reshape(NA_RB, GRID_W, D_MODEL)
        for j, bs in enumerate(NA_BAND_START):
            out_ref[0, :, j] = r3[:, bs:bs + NA_KBW, :].astype(BF16)


def _na_qkv(x, g, w):
    b, t, _ = x.shape
    rows = t // GRID_W
    tile = NA_RB * GRID_W
    band = jax.ShapeDtypeStruct((b, rows, NA_NCB, NA_KBW, D_MODEL), BF16)
    band_spec = pl.BlockSpec((1, NA_RB, NA_NCB, NA_KBW, D_MODEL), lambda i, r: (i, r, 0, 0, 0))
    return pl.pallas_call(
        _na_qkv_kernel,
        out_shape=(jax.ShapeDtypeStruct((b, t, D_MODEL), BF16), band, band),
        grid=(b, rows // NA_RB),
        in_specs=[pl.BlockSpec((1, tile, D_MODEL), lambda i, r: (i, r, 0)),
                  _resident((1, D_MODEL)),
                  _resident((D_MODEL, 3 * D_MODEL))],
        out_specs=(pl.BlockSpec((1, tile, D_MODEL), lambda i, r: (i, r, 0)), band_spec, band_spec),
        compiler_params=_cparams("parallel", "parallel"),
        name="na_qkv",
    )(x, g.reshape(1, D_MODEL), w)


def _na_key_block0(rb, nkb):
    return jnp.clip(2 * rb - 1, 0, nkb - NA_KROWS // NA_KSUB)


def _na_attn_kernel(q_ref, k0, k1, k2, k3, v0, v1, v2, v3, bias_ref, rmask_ref, o_ref, *, nkb):
    rb = pl.program_id(2)
    var = 2 * rb - _na_key_block0(rb, nkb)
    lo = lax.broadcasted_iota(jnp.int32, (1, LANE), 1) < NA_HEAD_DIM
    nq = NA_RB * NA_QBW
    for hp in range(NA_HEADS // 2):
        sl = slice(hp * LANE, (hp + 1) * LANE)
        qp = q_ref[0, :, :, sl].reshape(nq, LANE)
        zero = jnp.zeros_like(qp)
        ql = jnp.concatenate([jnp.where(lo, qp, zero), jnp.where(lo, zero, qp)], axis=0)
        kp = jnp.concatenate([r[0, :, 0, :, sl].reshape(NA_KSUB * NA_KBW, LANE) for r in (k0, k1, k2, k3)], axis=0)
        vp = jnp.concatenate([r[0, :, 0, :, sl].reshape(NA_KSUB * NA_KBW, LANE) for r in (v0, v1, v2, v3)], axis=0)
        s = _dot_nt(ql, kp)
        ps, ls = [], []
        for a in range(2):
            for qr in range(NA_RB):
                r0 = a * nq + qr * NA_QBW
                sc = s[r0:r0 + NA_QBW, :]
                bias = jnp.concatenate(
                    [bias_ref[0, 2 * hp + a, 4 * g - qr + 15 - 4 * var] + rmask_ref[var, qr, g]
                     for g in range(NA_KROWS // NA_KSUB)], axis=1)
                sc = sc + bias
                p = jnp.exp(sc - jnp.max(sc, axis=1, keepdims=True))
                ls.append(jnp.sum(p, axis=1, keepdims=True))
                ps.append(p.astype(BF16))
        o = _dot(jnp.concatenate(ps, axis=0), vp) / jnp.concatenate(ls, axis=0)
        out = jnp.where(lo, o[:nq], o[nq:]).astype(BF16)
        o_ref[0, :, :, sl] = out.reshape(NA_RB, NA_QBW, LANE)


def _na_tables(rpb):
    rpbp = jnp.pad(rpb.astype(F32), ((0, 0), (8, 8), (NA_KW, NA_KW)))
    strips = []
    for j in range(NA_NCB):
        for qc in range(NA_QBW):
            off = NA_BAND_START[j] - NA_QBW * j - qc + (NA_KW - 1) + NA_KW
            strips.append(rpbp[:, :, off:off + NA_KBW])
    strips = jnp.stack(strips).reshape(NA_NCB, NA_QBW, NA_HEADS, 2 * NA_KH - 1 + 16, NA_KBW)
    tab = jnp.concatenate([strips[:, :, :, i:i + NA_NE, :] for i in range(NA_KSUB)], axis=-1)
    tab = jnp.transpose(tab, (0, 2, 3, 1, 4))
    j = np.arange(NA_NCB)[:, None, None, None, None]
    qc = np.arange(NA_QBW)[None, None, None, :, None]
    ln = np.arange(LANE)[None, None, None, None, :]
    kcol = np.asarray(NA_BAND_START)[j] + ln % NA_KBW
    ws = np.clip(NA_QBW * j + qc - NA_KW // 2, 0, GRID_W - NA_KW)
    col_ok = (kcol >= ws) & (kcol < ws + NA_KW)
    tab = jnp.where(col_ok, tab, NEG)

    var = np.arange(3)[:, None, None, None]
    qr = np.arange(NA_RB)[None, :, None, None]
    g = np.arange(NA_KROWS // NA_KSUB)[None, None, :, None]
    kr = NA_KSUB * g + np.arange(LANE)[None, None, None, :] // NA_KBW
    rs = np.where(var == 0, np.maximum(qr - NA_KH // 2, 0),
                  np.where(var == 1, qr, np.minimum(qr + NA_KH // 2, NA_KH)))
    row_ok = (kr >= rs) & (kr < rs + NA_KH)
    rmask = np.where(row_ok, 0.0, NEG).astype(np.float32)
    rmask = np.broadcast_to(rmask[:, :, :, None, :], (3, NA_RB, NA_KROWS // NA_KSUB, NA_QBW, LANE))
    return tab, jnp.asarray(np.ascontiguousarray(rmask))


def _na_attn(q, kb, vb, rpb):
    b, t, _ = q.shape
    rows = t // GRID_W
    assert rows % NA_RB == 0 and rows >= NA_KROWS
    nkb = rows // NA_KSUB
    q4 = q.reshape(b, rows, GRID_W, D_MODEL)
    tab, rmask = _na_tables(rpb)
    q_spec = pl.BlockSpec((1, NA_RB, NA_QBW, D_MODEL), lambda j, i, r: (i, r, j, 0))

    def band_spec(off):
        return pl.BlockSpec((1, NA_KSUB, 1, NA_KBW, D_MODEL),
                            lambda j, i, r: (i, _na_key_block0(r, nkb) + off, j, 0, 0))

    o = pl.pallas_call(
        functools.partial(_na_attn_kernel, nkb=nkb),
        out_shape=jax.ShapeDtypeStruct((b, rows, GRID_W, D_MODEL), BF16),
        grid=(NA_NCB, b, rows // NA_RB),
        in_specs=[q_spec] + [band_spec(off) for off in range(4)] * 2 + [
            pl.BlockSpec((1, NA_HEADS, NA_NE, NA_QBW, LANE), lambda j, i, r: (j, 0, 0, 0, 0)),
            _resident(rmask.shape)],
        out_specs=q_spec,
        compiler_params=_cparams("parallel", "parallel", "parallel"),
        name="na_attn",
    )(q4, kb, kb, kb, kb, vb, vb, vb, vb, tab, rmask)
    return o.reshape(b, t, D_MODEL)


def _na_layer(x, g, w_qkv, rpb, w_o):
    b, t, _ = x.shape
    q, kb, vb = _na_qkv(x, g, w_qkv.astype(BF16))
    o = _na_attn(q, kb, vb, rpb)
    return _mm_res(o.reshape(b * t, D_MODEL), w_o.astype(BF16), x.reshape(b * t, D_MODEL))


def _lru_in_kernel(x_ref, g_ref, w_ref, gate_ref, xr_ref):
    xn = _rms(x_ref[...], g_ref[...]).astype(BF16)
    gate_ref[...] = _dot(xn, w_ref[:, 0:LRU_WIDTH])
    xr_ref[...] = _dot(xn, w_ref[:, LRU_WIDTH:2 * LRU_WIDTH])


def _lru_in(x, g, w):
    m = x.shape[0]
    out = jax.ShapeDtypeStruct((m, LRU_WIDTH), F32)
    spec = pl.BlockSpec((TOK_TILE, LRU_WIDTH), lambda i: (i, 0))
    return pl.pallas_call(
        _lru_in_kernel,
        out_shape=(out, out),
        grid=(m // TOK_TILE,),
        in_specs=[pl.BlockSpec((TOK_TILE, D_MODEL), lambda i: (i, 0)),
                  _resident((1, D_MODEL)),
                  _resident((D_MODEL, 2 * LRU_WIDTH))],
        out_specs=(spec, spec),
        compiler_params=_cparams("parallel"),
        name="lru_in",
    )(x, g.reshape(1, D_MODEL), w)


def _lru_scan_tile(xr_ref, prev_ref, next_ref, cw_ref, vec_ref, wg_ref,
                   ext_ref, a_ref, u_ref, h_ref, carry_ref, *, tile, n_tiles, reverse):
    step = pl.program_id(1)
    ti = (n_tiles - 1 - step) if reverse else step
    tt = tile

    @pl.when(step == 0)
    def _():
        carry_ref[...] = jnp.zeros_like(carry_ref)

    ext_ref[0:HALO, :] = jnp.where(ti > 0, prev_ref[0], 0.0)
    ext_ref[HALO:HALO + tt, :] = xr_ref[0]
    ext_ref[HALO + tt:2 * HALO + tt, :] = jnp.where(ti < n_tiles - 1, next_ref[0], 0.0)
    xc = vec_ref[0:1, :]
    for k in range(CONV_W):
        xc = xc + cw_ref[k:k + 1, :] * ext_ref[pl.ds(HALO - 2 + k, tt), :]
    xcb = xc.astype(BF16)
    r = jax.nn.sigmoid(_dot(xcb, wg_ref[:, 0:LRU_WIDTH]) + vec_ref[1:2, :])
    i = jax.nn.sigmoid(_dot(xcb, wg_ref[:, LRU_WIDTH:2 * LRU_WIDTH]) + vec_ref[2:3, :])
    z = -vec_ref[3:4, :]
    softplus = jnp.maximum(z, 0.0) + jnp.log1p(jnp.exp(-jnp.abs(z)))
    log_a = (-LRU_C * r) * softplus
    a = jnp.exp(log_a)
    u = jnp.sqrt(-jnp.tanh(log_a) * (a * a + 1.0)) * (i * xc)
    a_ref[...] = a
    u_ref[...] = u

    def body(k, h):
        t = (tt - 1 - k) if reverse else k
        h = a_ref[pl.ds(t, 1), :] * h + u_ref[pl.ds(t, 1), :]
        h_ref[pl.ds(t, 1), :] = h
        return h

    carry_ref[...] = lax.fori_loop(0, tt, body, carry_ref[...], unroll=8)


def _lru_bwd_kernel(xr_ref, prev_ref, next_ref, cw_ref, vec_ref, wg_ref, hb_ref,
                    ext_ref, a_ref, u_ref, h_ref, carry_ref, *, tile, n_tiles):
    _lru_scan_tile(xr_ref, prev_ref, next_ref, cw_ref, vec_ref, wg_ref,
                   ext_ref, a_ref, u_ref, h_ref, carry_ref, tile=tile, n_tiles=n_tiles, reverse=True)
    hb_ref[0] = h_ref[...]


def _lru_fwd_kernel(xr_ref, prev_ref, next_ref, cw_ref, vec_ref, wg_ref, hb_ref, gate_ref, res_ref, wo_ref,
                    o_ref, ext_ref, a_ref, u_ref, h_ref, carry_ref, *, tile, n_tiles):
    _lru_scan_tile(xr_ref, prev_ref, next_ref, cw_ref, vec_ref, wg_ref,
                   ext_ref, a_ref, u_ref, h_ref, carry_ref, tile=tile, n_tiles=n_tiles, reverse=False)
    y = jax.nn.gelu(gate_ref[0], approximate=True) * (h_ref[...] + hb_ref[0])
    o_ref[0] = res_ref[0] + _dot(y.astype(BF16), wo_ref[...])


def _block_diag(w):
    eye = jnp.eye(LRU_BLOCKS, dtype=w.dtype)
    return jnp.einsum('ncd,nm->ncmd', w, eye).reshape(LRU_WIDTH, LRU_WIDTH)


def _lru_layer(x, g, w_in, conv_w, conv_b, w_a, b_a, w_x, b_x, lam, w_out):
    b, t, _ = x.shape
    c = LRU_WIDTH
    tile = TOK_TILE
    n_tiles = t // tile
    hpt = tile // HALO
    gate, xr = _lru_in(x.reshape(b * t, D_MODEL), g, w_in.astype(BF16))
    gate = gate.reshape(b, t, c)
    xr = xr.reshape(b, t, c)

    def tile_idx(s, reverse):
        return (n_tiles - 1 - s) if reverse else s

    def scan_specs(reverse):
        return [
            pl.BlockSpec((1, tile, c), lambda i, s: (i, tile_idx(s, reverse), 0)),
            pl.BlockSpec((1, HALO, c), lambda i, s: (i, jnp.maximum(tile_idx(s, reverse) * hpt - 1, 0), 0)),
            pl.BlockSpec((1, HALO, c),
                         lambda i, s: (i, jnp.minimum((tile_idx(s, reverse) + 1) * hpt, t // HALO - 1), 0)),
            _resident((CONV_W, c)),
            _resident((4, c)),
            _resident((c, 2 * c)),
        ]

    scratch = [pltpu.VMEM((tile + 2 * HALO, c), F32), pltpu.VMEM((tile, c), F32), pltpu.VMEM((tile, c), F32),
               pltpu.VMEM((tile, c), F32), pltpu.VMEM((1, c), F32)]

    def direction_params(d):
        vec = jnp.stack([conv_b, b_a[d], b_x[d], lam[d]]).astype(F32)
        wg = jnp.concatenate([_block_diag(w_a[d]), _block_diag(w_x[d])], axis=1).astype(BF16)
        return vec, wg

    vec1, wg1 = direction_params(1)
    hb = pl.pallas_call(
        functools.partial(_lru_bwd_kernel, tile=tile, n_tiles=n_tiles),
        out_shape=jax.ShapeDtypeStruct((b, t, c), F32),
        grid=(b, n_tiles),
        in_specs=scan_specs(True),
        out_specs=pl.BlockSpec((1, tile, c), lambda i, s: (i, tile_idx(s, True), 0)),
        scratch_shapes=scratch,
        compiler_params=_cparams("arbitrary", "arbitrary"),
        name="lru_bwd",
    )(xr, xr, xr, conv_w, vec1, wg1)

    vec0, wg0 = direction_params(0)
    tok = lambda n: pl.BlockSpec((1, tile, n), lambda i, s: (i, s, 0))
    return pl.pallas_call(
        functools.partial(_lru_fwd_kernel, tile=tile, n_tiles=n_tiles),
        out_shape=jax.ShapeDtypeStruct((b, t, D_MODEL), F32),
        grid=(b, n_tiles),
        in_specs=scan_specs(False) + [tok(c), tok(c), tok(D_MODEL), _resident((c, D_MODEL))],
        out_specs=tok(D_MODEL),
        scratch_shapes=scratch,
        compiler_params=_cparams("arbitrary", "arbitrary"),
        name="lru_fwd",
    )(xr, xr, xr, conv_w, vec0, wg0, hb, gate, x, w_out.astype(BF16)).reshape(b * t, D_MODEL)


def _rope_lanes(v, cos, sin):
    return v * cos + pltpu.roll(v, LANE // 2, 1) * sin


def _mla_proj_kernel(x_ref, g_ref, wdq_ref, gq_ref, wuq_ref, wdkv_ref, gkv_ref, wukv_ref, cos_ref, sin_ref,
                     qn_ref, qr_ref, kn_ref, kr_ref, v_ref):
    scale = (MLA_NOPE + MLA_ROPE) ** -0.5 * float(np.log2(np.e))
    nq = MLA_HEADS * MLA_NOPE
    xn = _rms(x_ref[0], g_ref[...]).astype(BF16)
    cos = cos_ref[...]
    sin = sin_ref[...]
    cq = _rms(_dot(xn, wdq_ref[...]), gq_ref[...]).astype(BF16)
    q = _dot(cq, wuq_ref[...])
    qn_ref[0] = (q[:, 0:nq] * scale).astype(BF16)
    for p in range(MLA_HEADS // 2):
        sl = slice(nq + p * LANE, nq + (p + 1) * LANE)
        qr_ref[0, :, p * LANE:(p + 1) * LANE] = (_rope_lanes(q[:, sl], cos, sin) * scale).astype(BF16)
    kva = _dot(xn, wdkv_ref[...])
    kr_ref[0] = _rope_lanes(kva[:, MLA_KV_RANK:MLA_KV_RANK + LANE], cos, sin).astype(BF16)
    ckv = _rms(kva[:, 0:MLA_KV_RANK], gkv_ref[...]).astype(BF16)
    kv = _dot(ckv, wukv_ref[...])
    kn_ref[0] = kv[:, 0:nq].astype(BF16)
    v_ref[0] = kv[:, nq:2 * nq].astype(BF16)


def _mla_attn_kernel(qn_ref, qr_ref, kn_ref, kr_ref, v_ref, o_ref,
                     ql_ref, s_ref, p_ref, mxb_ref, m_ref, l_ref, acc_ref, *, seq):
    tq, tk = MLA_TQ, MLA_TK
    rows = 2 * tq
    n = seq // tk
    assert n >= 2 and n % 2 == 0
    qp = jnp.concatenate([qn_ref[0], qr_ref[0]], axis=1)
    lane = lax.broadcasted_iota(jnp.int32, (1, 2 * LANE), 1)
    in_a = (lane < MLA_NOPE) | ((lane >= LANE) & (lane < LANE + MLA_ROPE))
    in_b = ((lane >= MLA_NOPE) & (lane < LANE)) | ((lane >= LANE + MLA_ROPE) & (lane < LANE + 2 * MLA_ROPE))
    zero = jnp.zeros_like(qp)
    ql_ref[0:tq, :] = jnp.where(in_a, qp, zero)
    ql_ref[tq:rows, :] = jnp.where(in_b, qp, zero)
    m_ref[...] = jnp.full(m_ref.shape, NEG, F32)
    l_ref[...] = jnp.zeros(l_ref.shape, F32)
    acc_ref[...] = jnp.zeros(acc_ref.shape, F32)

    def key_rows(kt):
        return pl.ds(kt * tk if isinstance(kt, int) else pl.multiple_of(kt * tk, tk), tk)

    def qk(kt, slot):
        kk = jnp.concatenate([kn_ref[0, key_rows(kt), :], kr_ref[0, key_rows(kt), :]], axis=1)
        s = _dot_nt(ql_ref[...], kk)
        s_ref[slot] = s
        mxb_ref[slot] = jnp.broadcast_to(jnp.max(s, axis=1, keepdims=True), (rows, LANE))

    def pv(kt, slot):
        return _dot(p_ref[slot], v_ref[0, key_rows(kt), :])

    def stage(kt, slot, qk_next, pv_prev):
        if qk_next:
            qk(kt + 1, 1 - slot)
        o_prev = pv(kt - 1, 1 - slot) if pv_prev else None
        m_old = m_ref[...]
        m_new = jnp.maximum(m_old, mxb_ref[slot])
        alpha = jnp.exp2(m_old - m_new)
        m_ref[...] = m_new
        for c in range(rows // MLA_CH):
            r = slice(c * MLA_CH, (c + 1) * MLA_CH)
            ps = [jnp.exp2(s_ref[slot, r, g * LANE:(g + 1) * LANE] - m_new[r, :]) for g in range(tk // LANE)]
            l_ref[r, :] = alpha[r, :] * l_ref[r, :] + functools.reduce(lambda a, b: a + b, ps)
            p_ref[slot, r, :] = jnp.concatenate(ps, axis=1).astype(BF16)
            if pv_prev:
                acc_ref[r, :] = alpha[r, :] * (acc_ref[r, :] + o_prev[r, :])

    qk(0, 0)
    stage(0, 0, True, False)

    def pair(i, carry):
        kt = 1 + 2 * i
        stage(kt, 1, True, True)
        stage(kt + 1, 0, True, True)
        return carry

    lax.fori_loop(0, (n - 2) // 2, pair, 0)
    stage(n - 1, 1, False, True)
    o = (acc_ref[...] + pv(n - 1, 1)) / jnp.sum(l_ref[...], axis=1, keepdims=True)
    lo = lax.broadcasted_iota(jnp.int32, (1, LANE), 1) < MLA_V
    o_ref[0] = jnp.where(lo, o[:tq], o[tq:]).astype(BF16)


def _mla_weights(w_uq, w_dkv, w_ukv):
    h, nope, rope = MLA_HEADS, MLA_NOPE, MLA_ROPE
    half = rope // 2
    wq = w_uq.reshape(MLA_Q_RANK, h, nope + rope)
    wq_nope = wq[:, :, :nope].reshape(MLA_Q_RANK, h * nope)
    wq_rope = wq[:, :, nope:].reshape(MLA_Q_RANK, h // 2, 2, rope)
    wq_swap = jnp.concatenate([wq_rope[..., half:], wq_rope[..., :half]], axis=-1)
    wq_rope = jnp.concatenate([wq_rope.reshape(MLA_Q_RANK, h // 2, 2 * rope),
                               wq_swap.reshape(MLA_Q_RANK, h // 2, 2 * rope)], axis=-1)
    wuq = jnp.concatenate([wq_nope, wq_rope.reshape(MLA_Q_RANK, h // 2 * LANE)], axis=1)
    w_kr = w_dkv[:, MLA_KV_RANK:]
    w_krs = jnp.concatenate([w_kr[:, half:], w_kr[:, :half]], axis=1)
    wdkv = jnp.concatenate([w_dkv[:, :MLA_KV_RANK], w_kr, w_kr, w_krs, w_krs], axis=1)
    wkv = w_ukv.reshape(MLA_KV_RANK, h, nope + MLA_V)
    wukv = jnp.concatenate([wkv[:, :, :nope].reshape(MLA_KV_RANK, h * nope),
                            wkv[:, :, nope:].reshape(MLA_KV_RANK, h * MLA_V)], axis=1)
    return wuq.astype(BF16), wdkv.astype(BF16), wukv.astype(BF16)


def _rope_tables(t):
    pos = jnp.arange(t, dtype=F32)
    inv = ROPE_THETA ** (-jnp.arange(0, MLA_ROPE, 2, dtype=F32) / MLA_ROPE)
    ang = pos[:, None] * inv[None, :]
    cos, sin = jnp.cos(ang), jnp.sin(ang)
    pad = jnp.zeros((t, LANE // 2), F32)
    return (jnp.concatenate([cos, cos, cos, cos, pad], axis=1),
            jnp.concatenate([-sin, sin, -sin, sin, pad], axis=1))


def _mla_layer(x, g, w_dq, g_q, w_uq, w_dkv, g_kv, w_ukv, w_o):
    b, t, _ = x.shape
    tm = TOK_TILE
    wuq, wdkv, wukv = _mla_weights(w_uq, w_dkv, w_ukv)
    cos, sin = _rope_tables(t)
    wide = MLA_HEADS * MLA_NOPE
    tok = lambda n: pl.BlockSpec((1, tm, n), lambda i, s: (i, s, 0))
    big = jax.ShapeDtypeStruct((b, t, wide), BF16)
    qn, qr, kn, kr, v = pl.pallas_call(
        _mla_proj_kernel,
        out_shape=(big, big, big, jax.ShapeDtypeStruct((b, t, LANE), BF16), big),
        grid=(b, t // tm),
        in_specs=[tok(D_MODEL), _resident((1, D_MODEL)),
                  _resident((D_MODEL, MLA_Q_RANK)), _resident((1, MLA_Q_RANK)), _resident(wuq.shape),
                  _resident(wdkv.shape), _resident((1, MLA_KV_RANK)), _resident(wukv.shape),
                  pl.BlockSpec((tm, LANE), lambda i, s: (s, 0)), pl.BlockSpec((tm, LANE), lambda i, s: (s, 0))],
        out_specs=(tok(wide), tok(wide), tok(wide), tok(LANE), tok(wide)),
        compiler_params=_cparams("parallel", "parallel"),
        name="mla_proj",
    )(x, g.reshape(1, D_MODEL), w_dq.astype(BF16), g_q.reshape(1, MLA_Q_RANK), wuq,
      wdkv, g_kv.reshape(1, MLA_KV_RANK), wukv, cos, sin)

    q_spec = pl.BlockSpec((1, MLA_TQ, LANE), lambda i, p, s: (i, s, p))
    kv_spec = pl.BlockSpec((1, t, LANE), lambda i, p, s: (i, 0, p))
    o = pl.pallas_call(
        functools.partial(_mla_attn_kernel, seq=t),
        out_shape=jax.ShapeDtypeStruct((b, t, MLA_HEADS * MLA_V), BF16),
        grid=(b, MLA_HEADS // 2, t // MLA_TQ),
        in_specs=[q_spec, q_spec, kv_spec,
                  pl.BlockSpec((1, t, LANE), lambda i, p, s: (i, 0, 0)), kv_spec],
        out_specs=q_spec,
        scratch_shapes=[pltpu.VMEM((2 * MLA_TQ, 2 * LANE), BF16),
                        pltpu.VMEM((2, 2 * MLA_TQ, MLA_TK), F32),
                        pltpu.VMEM((2, 2 * MLA_TQ, MLA_TK), BF16),
                        pltpu.VMEM((2, 2 * MLA_TQ, LANE), F32),
                        pltpu.VMEM((2 * MLA_TQ, LANE), F32),
                        pltpu.VMEM((2 * MLA_TQ, LANE), F32),
                        pltpu.VMEM((2 * MLA_TQ, LANE), F32)],
        compiler_params=_cparams("parallel", "parallel", "parallel"),
        name="mla_attn",
    )(qn, qr, kn, kr, v)
    return _mm_res(o.reshape(b * t, MLA_HEADS * MLA_V), w_o.astype(BF16), x.reshape(b * t, D_MODEL))


def kernel(x_prompt, x_sample, norm_mix, norm_ffn, norm_final, na_w_qkv, na_rpb, na_w_o, lru_w_in, lru_conv_w, lru_conv_b, lru_w_a, lru_b_a, lru_w_x, lru_b_x, lru_lam, lru_w_out, mla_w_dq, mla_g_q, mla_w_uq, mla_w_dkv, mla_g_kv, mla_w_ukv, mla_w_o, ffn_w_gate, ffn_w_up, ffn_w_down):
    wg = ffn_w_gate.astype(BF16)
    wu = ffn_w_up.astype(BF16)
    wd = ffn_w_down.astype(BF16)

    def run(x):
        b, t, _ = x.shape
        for i in range(DEPTH):
            j, m = divmod(i, N_MIXERS)
            x3 = x.reshape(b, t, D_MODEL)
            if m == 0:
                x = _na_layer(x3, norm_mix[i], na_w_qkv[j], na_rpb[j], na_w_o[j])
            elif m == 1:
                x = _lru_layer(x3, norm_mix[i], lru_w_in[j], lru_conv_w[j], lru_conv_b[j], lru_w_a[j], lru_b_a[j],
                               lru_w_x[j], lru_b_x[j], lru_lam[j], lru_w_out[j])
            else:
                x = _mla_layer(x3, norm_mix[i], mla_w_dq[j], mla_g_q[j], mla_w_uq[j], mla_w_dkv[j], mla_g_kv[j],
                               mla_w_ukv[j], mla_w_o[j])
            x = _ffn(x, norm_ffn[i], wg[i], wu[i], wd[i], norm_final if i == DEPTH - 1 else None)
        return x.reshape(b, t, D_MODEL)

    return run(x_prompt), run(x_sample)
```

```python
import functools

import numpy as np
import jax
import jax.numpy as jnp
from jax import lax
from jax.experimental import pallas as pl
from jax.experimental.pallas import tpu as pltpu

F32 = jnp.float32
BF16 = jnp.bfloat16

D_MODEL = 1024
DEPTH = 4
N_MIXERS = 3
RMS_EPS = 1e-6
D_FF = 2816

GRID_W = 64
NA_HEADS = 16
NA_HEAD_DIM = 64
NA_KH = 8
NA_KW = 16
NA_QBW = 16
NA_KBW = 32
NA_NCB = GRID_W // NA_QBW
NA_BAND_START = tuple(int(v) for v in np.clip(np.arange(NA_NCB) * NA_QBW - NA_KW // 2, 0, GRID_W - NA_KBW))
NA_RB = 8
NA_KROWS = 16
NA_KSUB = 4
NA_NE = 28
NEG = -1e30
LOG2E = float(np.log2(np.e))

LRU_WIDTH = 1408
LRU_BLOCKS = 16
LRU_BLOCK = 88
CONV_W = 4
LRU_C = 8.0
HALO = 8
LRU_NTILE = 256
LRU_KWIN = 512
LRU_BAND_START = tuple(min(max(c * LRU_NTILE - 128, 0), LRU_WIDTH - LRU_KWIN)
                       for c in range(-(-LRU_WIDTH // LRU_NTILE)))

MLA_HEADS = 16
MLA_Q_RANK = 384
MLA_KV_RANK = 256
MLA_NOPE = 64
MLA_ROPE = 32
MLA_V = 64
ROPE_THETA = 10000.0

LANE = 128
TOK_TILE = 512
FF_CHUNK = 256
MLA_TQ = 256
MLA_TK = 512
MLA_CH = 32
VMEM_LIMIT = 56 * 1024 * 1024


def _cparams(*sem):
    return pltpu.CompilerParams(dimension_semantics=sem, vmem_limit_bytes=VMEM_LIMIT)


def _resident(shape):
    nd = len(shape)
    return pl.BlockSpec(shape, lambda *_: (0,) * nd, pipeline_mode=pl.Buffered(1))


def _rms(x, g):
    return x * lax.rsqrt(jnp.mean(x * x, axis=-1, keepdims=True) + RMS_EPS) * g


def _dot(a, b):
    return jnp.dot(a, b, preferred_element_type=F32)


def _dot_nt(a, b):
    return lax.dot_general(a, b, (((1,), (1,)), ((), ())), preferred_element_type=F32)


def _ffn_kernel(*refs, mixer, final):
    x_ref, g_ref, wg_ref, wu_ref, wd_ref = refs[:5]
    o_ref = refs[-1]
    x = x_ref[...]
    if mixer:
        x = x + _dot(refs[5][...], refs[6][...])
    xn = _rms(x, g_ref[...]).astype(BF16)
    acc = x
    for c in range(D_FF // FF_CHUNK):
        sl = slice(c * FF_CHUNK, (c + 1) * FF_CHUNK)
        g = _dot(xn, wg_ref[:, sl])
        u = _dot(xn, wu_ref[:, sl])
        h = (g * jax.nn.sigmoid(g) * u).astype(BF16)
        acc = acc + _dot(h, wd_ref[sl, :])
    if final:
        acc = _rms(acc, refs[-2][...])
    o_ref[...] = acc


def _ffn(x, g, wg, wu, wd, g_final=None, attn=None, w_o=None):
    m = x.shape[0]
    final = g_final is not None
    mixer = attn is not None
    ins = [x, g.reshape(1, D_MODEL), wg, wu, wd]
    specs = [pl.BlockSpec((TOK_TILE, D_MODEL), lambda i: (i, 0)),
             _resident((1, D_MODEL)),
             _resident((D_MODEL, D_FF)), _resident((D_MODEL, D_FF)), _resident((D_FF, D_MODEL))]
    if mixer:
        ins += [attn, w_o]
        specs += [pl.BlockSpec((TOK_TILE, attn.shape[1]), lambda i: (i, 0)), _resident(w_o.shape)]
    if final:
        ins.append(g_final.reshape(1, D_MODEL))
        specs.append(_resident((1, D_MODEL)))
    return pl.pallas_call(
        functools.partial(_ffn_kernel, mixer=mixer, final=final),
        out_shape=jax.ShapeDtypeStruct((m, D_MODEL), F32),
        grid=(m // TOK_TILE,),
        in_specs=specs,
        out_specs=pl.BlockSpec((TOK_TILE, D_MODEL), lambda i: (i, 0)),
        compiler_params=_cparams("parallel"),
        name="ffn",
    )(*ins)


def _na_qkv_kernel(x_ref, g_ref, w_ref, q_ref, kb_ref, vb_ref):
    xn = _rms(x_ref[0], g_ref[...]).astype(BF16)
    q = _dot(xn, w_ref[:, 0:D_MODEL])
    q_ref[0] = (q * (NA_HEAD_DIM ** -0.5 * LOG2E)).astype(BF16)
    for part, out_ref in ((1, kb_ref), (2, vb_ref)):
        r = _dot(xn, w_ref[:, part * D_MODEL:(part + 1) * D_MODEL])
        r3 = r.reshape(NA_RB, GRID_W, D_MODEL)
        for j, bs in enumerate(NA_BAND_START):
            out_ref[0, :, j] = r3[:, bs:bs + NA_KBW, :].astype(BF16)


def _na_qkv(x, g, w):
    b, t, _ = x.shape
    rows = t // GRID_W
    tile = NA_RB * GRID_W
    band = jax.ShapeDtypeStruct((b, rows, NA_NCB, NA_KBW, D_MODEL), BF16)
    band_spec = pl.BlockSpec((1, NA_RB, NA_NCB, NA_KBW, D_MODEL), lambda i, r: (i, r, 0, 0, 0))
    return pl.pallas_call(
        _na_qkv_kernel,
        out_shape=(jax.ShapeDtypeStruct((b, t, D_MODEL), BF16), band, band),
        grid=(b, rows // NA_RB),
        in_specs=[pl.BlockSpec((1, tile, D_MODEL), lambda i, r: (i, r, 0)),
                  _resident((1, D_MODEL)),
                  _resident((D_MODEL, 3 * D_MODEL))],
        out_specs=(pl.BlockSpec((1, tile, D_MODEL), lambda i, r: (i, r, 0)), band_spec, band_spec),
        compiler_params=_cparams("parallel", "parallel"),
        name="na_qkv",
    )(x, g.reshape(1, D_MODEL), w)


def _na_key_block0(rb, nkb):
    return jnp.clip(2 * rb - 1, 0, nkb - NA_KROWS // NA_KSUB)


def _na_first_key_row(var, qr):
    half = NA_KH // 2
    return (max(qr - half, 0), qr, min(qr + half, NA_KH))[var]


def _na_attn_body(var, q_ref, k_refs, v_refs, bias_ref, rmask_ref, o_ref):
    lo = lax.broadcasted_iota(jnp.int32, (1, LANE), 1) < NA_HEAD_DIM
    nq = NA_RB * NA_QBW
    ng = NA_KROWS // NA_KSUB
    zero_tile = jnp.zeros((NA_QBW, LANE), BF16)
    for hp in range(NA_HEADS // 2):
        sl = slice(hp * LANE, (hp + 1) * LANE)
        qp = q_ref[0, :, :, sl].reshape(nq, LANE)
        zero = jnp.zeros_like(qp)
        ql = jnp.concatenate([jnp.where(lo, qp, zero), jnp.where(lo, zero, qp)], axis=0)
        kp = jnp.concatenate([r[0, :, 0, :, sl].reshape(NA_KSUB * NA_KBW, LANE) for r in k_refs], axis=0)
        vp = jnp.concatenate([r[0, :, 0, :, sl].reshape(NA_KSUB * NA_KBW, LANE) for r in v_refs], axis=0)
        s = _dot_nt(ql, kp)
        chunks = []
        for a in range(2):
            for qr in range(NA_RB):
                r0 = a * nq + qr * NA_QBW
                rs = _na_first_key_row(var, qr)
                tiles = {}
                for g in range(ng):
                    if NA_KSUB * (g + 1) > rs and NA_KSUB * g < rs + NA_KH:
                        t = (s[r0:r0 + NA_QBW, g * LANE:(g + 1) * LANE]
                             + bias_ref[0, 2 * hp + a, 4 * g - qr + 15 - 4 * var])
                        if not (NA_KSUB * g >= rs and NA_KSUB * (g + 1) <= rs + NA_KH):
                            t = t + rmask_ref[var, qr, g]
                        tiles[g] = t
                chunks.append(tiles)
        lane_max = jnp.concatenate([functools.reduce(jnp.maximum, tiles.values()) for tiles in chunks], axis=0)
        mb = jnp.broadcast_to(jnp.max(lane_max, axis=1, keepdims=True), (2 * nq, LANE))
        ps = []
        for ci, tiles in enumerate(chunks):
            m = mb[ci * NA_QBW:(ci + 1) * NA_QBW, :]
            ps.append(jnp.concatenate(
                [jnp.exp2(tiles[g] - m).astype(BF16) if g in tiles else zero_tile for g in range(ng)], axis=1))
        o2 = _dot(jnp.concatenate(ps, axis=0), jnp.concatenate([vp, jnp.ones_like(vp)], axis=1))
        o = o2[:, 0:LANE] / o2[:, LANE:2 * LANE]
        out = jnp.where(lo, o[:nq], o[nq:]).astype(BF16)
        o_ref[0, :, :, sl] = out.reshape(NA_RB, NA_QBW, LANE)


def _na_attn_kernel(q_ref, k0, k1, k2, k3, v0, v1, v2, v3, bias_ref, rmask_ref, o_ref, *, nkb):
    rb = pl.program_id(2)
    kind = 2 * rb - _na_key_block0(rb, nkb)
    for var in range(3):
        pl.when(kind == var)(functools.partial(
            _na_attn_body, var, q_ref, (k0, k1, k2, k3), (v0, v1, v2, v3), bias_ref, rmask_ref, o_ref))


def _na_tables(rpb):
    rpbp = jnp.pad(rpb.astype(F32) * LOG2E, ((0, 0), (8, 8), (NA_KW, NA_KW)))
    strips = []
    for j in range(NA_NCB):
        for qc in range(NA_QBW):
            off = NA_BAND_START[j] - NA_QBW * j - qc + (NA_KW - 1) + NA_KW
            strips.append(rpbp[:, :, off:off + NA_KBW])
    strips = jnp.stack(strips).reshape(NA_NCB, NA_QBW, NA_HEADS, 2 * NA_KH - 1 + 16, NA_KBW)
    tab = jnp.concatenate([strips[:, :, :, i:i + NA_NE, :] for i in range(NA_KSUB)], axis=-1)
    tab = jnp.transpose(tab, (0, 2, 3, 1, 4))
    j = np.arange(NA_NCB)[:, None, None, None, None]
    qc = np.arange(NA_QBW)[None, None, None, :, None]
    ln = np.arange(LANE)[None, None, None, None, :]
    kcol = np.asarray(NA_BAND_START)[j] + ln % NA_KBW
    ws = np.clip(NA_QBW * j + qc - NA_KW // 2, 0, GRID_W - NA_KW)
    col_ok = (kcol >= ws) & (kcol < ws + NA_KW)
    tab = jnp.where(col_ok, tab, NEG)

    var = np.arange(3)[:, None, None, None]
    qr = np.arange(NA_RB)[None, :, None, None]
    g = np.arange(NA_KROWS // NA_KSUB)[None, None, :, None]
    kr = NA_KSUB * g + np.arange(LANE)[None, None, None, :] // NA_KBW
    rs = np.where(var == 0, np.maximum(qr - NA_KH // 2, 0),
                  np.where(var == 1, qr, np.minimum(qr + NA_KH // 2, NA_KH)))
    row_ok = (kr >= rs) & (kr < rs + NA_KH)
    rmask = np.where(row_ok, 0.0, NEG).astype(np.float32)
    rmask = np.broadcast_to(rmask[:, :, :, None, :], (3, NA_RB, NA_KROWS // NA_KSUB, NA_QBW, LANE))
    return tab, jnp.asarray(np.ascontiguousarray(rmask))


def _na_attn(q, kb, vb, rpb):
    b, t, _ = q.shape
    rows = t // GRID_W
    assert rows % NA_RB == 0 and rows >= NA_KROWS
    nkb = rows // NA_KSUB
    q4 = q.reshape(b, rows, GRID_W, D_MODEL)
    tab, rmask = _na_tables(rpb)
    q_spec = pl.BlockSpec((1, NA_RB, NA_QBW, D_MODEL), lambda j, i, r: (i, r, j, 0))

    def band_spec(off):
        return pl.BlockSpec((1, NA_KSUB, 1, NA_KBW, D_MODEL),
                            lambda j, i, r: (i, _na_key_block0(r, nkb) + off, j, 0, 0))

    o = pl.pallas_call(
        functools.partial(_na_attn_kernel, nkb=nkb),
        out_shape=jax.ShapeDtypeStruct((b, rows, GRID_W, D_MODEL), BF16),
        grid=(NA_NCB, b, rows // NA_RB),
        in_specs=[q_spec] + [band_spec(off) for off in range(4)] * 2 + [
            pl.BlockSpec((1, NA_HEADS, NA_NE, NA_QBW, LANE), lambda j, i, r: (j, 0, 0, 0, 0)),
            _resident(rmask.shape)],
        out_specs=q_spec,
        compiler_params=_cparams("parallel", "parallel", "parallel"),
        name="na_attn",
    )(q4, kb, kb, kb, kb, vb, vb, vb, vb, tab, rmask)
    return o.reshape(b, t, D_MODEL)


def _na_layer(x, g, w_qkv, rpb, w_o):
    b, t, _ = x.shape
    q, kb, vb = _na_qkv(x, g, w_qkv.astype(BF16))
    return _na_attn(q, kb, vb, rpb).reshape(b * t, D_MODEL), w_o.astype(BF16)


def _lru_in_kernel(x_ref, g_ref, w_ref, gate_ref, xr_ref):
    xn = _rms(x_ref[...], g_ref[...]).astype(BF16)
    gate_ref[...] = _dot(xn, w_ref[:, 0:LRU_WIDTH])
    xr_ref[...] = _dot(xn, w_ref[:, LRU_WIDTH:2 * LRU_WIDTH])


def _lru_in(x, g, w):
    m = x.shape[0]
    out = jax.ShapeDtypeStruct((m, LRU_WIDTH), F32)
    spec = pl.BlockSpec((TOK_TILE, LRU_WIDTH), lambda i: (i, 0))
    return pl.pallas_call(
        _lru_in_kernel,
        out_shape=(out, out),
        grid=(m // TOK_TILE,),
        in_specs=[pl.BlockSpec((TOK_TILE, D_MODEL), lambda i: (i, 0)),
                  _resident((1, D_MODEL)),
                  _resident((D_MODEL, 2 * LRU_WIDTH))],
        out_specs=(spec, spec),
        compiler_params=_cparams("parallel"),
        name="lru_in",
    )(x, g.reshape(1, D_MODEL), w)


def _lru_scan_tile(xr_ref, prev_ref, next_ref, cw_ref, vec_ref, wa_ref, wx_ref,
                   a_ref, u_ref, h_ref, carry_ref, *, tile, n_tiles, reverse):
    step = pl.program_id(1)
    ti = (n_tiles - 1 - step) if reverse else step
    tt = tile

    @pl.when(step == 0)
    def _():
        carry_ref[...] = jnp.zeros_like(carry_ref)

    xr = xr_ref[0]
    n_ext = tt + 2 * HALO
    ext = jnp.concatenate([jnp.where(ti > 0, prev_ref[0], 0.0), xr,
                           jnp.where(ti < n_tiles - 1, next_ref[0], 0.0)], axis=0)
    xc = vec_ref[0:1, :] + cw_ref[2:3, :] * xr
    for k in (0, 1, 3):
        xc = xc + cw_ref[k:k + 1, :] * pltpu.roll(ext, (2 - k) % n_ext, 0)[HALO:HALO + tt, :]
    xcb = xc.astype(BF16)
    z = -vec_ref[3:4, :]
    c_pos = 0.5 * LRU_C * (jnp.maximum(z, 0.0) + jnp.log1p(jnp.exp(-jnp.abs(z))))
    c_exp = c_pos * (-LOG2E)
    for c, k0 in enumerate(LRU_BAND_START):
        w = min(LRU_NTILE, LRU_WIDTH - c * LRU_NTILE)
        cols = slice(c * LRU_NTILE, c * LRU_NTILE + w)
        xw = xcb[:, k0:k0 + LRU_KWIN]
        t_a = jnp.tanh(_dot(xw, wa_ref[c, :, 0:w]) + 0.5 * vec_ref[1:2, cols])
        t_x = jnp.tanh(_dot(xw, wx_ref[c, :, 0:w]) + 0.5 * vec_ref[2:3, cols])
        t1 = t_a + 1.0
        a = jnp.exp2(t1 * c_exp[:, cols])
        th = jnp.tanh(t1 * c_pos[:, cols])
        v = th * (a * a + 1.0)
        u = jnp.where(v > 0.0, v * lax.rsqrt(v), 0.0) * ((t_x + 1.0) * (0.5 * xc[:, cols]))
        a_ref[:, cols] = a
        u_ref[:, cols] = u

    def body(k, h):
        t = (tt - 1 - k) if reverse else k
        h = a_ref[pl.ds(t, 1), :] * h + u_ref[pl.ds(t, 1), :]
        h_ref[pl.ds(t, 1), :] = h
        return h

    carry_ref[...] = lax.fori_loop(0, tt, body, carry_ref[...], unroll=8)


def _lru_bwd_kernel(xr_ref, prev_ref, next_ref, cw_ref, vec_ref, wa_ref, wx_ref, hb_ref,
                    a_ref, u_ref, h_ref, carry_ref, *, tile, n_tiles):
    _lru_scan_tile(xr_ref, prev_ref, next_ref, cw_ref, vec_ref, wa_ref, wx_ref,
                   a_ref, u_ref, h_ref, carry_ref, tile=tile, n_tiles=n_tiles, reverse=True)
    hb_ref[0] = h_ref[...]


def _lru_fwd_kernel(xr_ref, prev_ref, next_ref, cw_ref, vec_ref, wa_ref, wx_ref, hb_ref, gate_ref, res_ref, wo_ref,
                    o_ref, a_ref, u_ref, h_ref, carry_ref, *, tile, n_tiles):
    _lru_scan_tile(xr_ref, prev_ref, next_ref, cw_ref, vec_ref, wa_ref, wx_ref,
                   a_ref, u_ref, h_ref, carry_ref, tile=tile, n_tiles=n_tiles, reverse=False)
    y = jax.nn.gelu(gate_ref[0], approximate=True) * (h_ref[...] + hb_ref[0])
    o_ref[0] = res_ref[0] + _dot(y.astype(BF16), wo_ref[...])


def _gate_bands(w):
    for c, k0 in enumerate(LRU_BAND_START):
        first = (c * LRU_NTILE) // LRU_BLOCK
        last = (min((c + 1) * LRU_NTILE, LRU_WIDTH) - 1) // LRU_BLOCK
        assert k0 <= first * LRU_BLOCK and (last + 1) * LRU_BLOCK <= k0 + LRU_KWIN
    eye = jnp.eye(LRU_BLOCKS, dtype=w.dtype)
    dense = jnp.einsum('ncd,nm->ncmd', 0.5 * w, eye).reshape(LRU_WIDTH, LRU_WIDTH)
    dense = jnp.pad(dense, ((0, 0), (0, len(LRU_BAND_START) * LRU_NTILE - LRU_WIDTH)))
    return jnp.stack([dense[k0:k0 + LRU_KWIN, c * LRU_NTILE:(c + 1) * LRU_NTILE]
                      for c, k0 in enumerate(LRU_BAND_START)]).astype(BF16)


def _lru_layer(x, g, w_in, conv_w, conv_b, w_a, b_a, w_x, b_x, lam, w_out):
    b, t, _ = x.shape
    c = LRU_WIDTH
    tile = TOK_TILE
    n_tiles = t // tile
    hpt = tile // HALO
    gate, xr = _lru_in(x.reshape(b * t, D_MODEL), g, w_in.astype(BF16))
    gate = gate.reshape(b, t, c)
    xr = xr.reshape(b, t, c)

    def tile_idx(s, reverse):
        return (n_tiles - 1 - s) if reverse else s

    def scan_specs(reverse):
        return [
            pl.BlockSpec((1, tile, c), lambda i, s: (i, tile_idx(s, reverse), 0)),
            pl.BlockSpec((1, HALO, c), lambda i, s: (i, jnp.maximum(tile_idx(s, reverse) * hpt - 1, 0), 0)),
            pl.BlockSpec((1, HALO, c),
                         lambda i, s: (i, jnp.minimum((tile_idx(s, reverse) + 1) * hpt, t // HALO - 1), 0)),
            _resident((CONV_W, c)),
            _resident((4, c)),
            _resident(band_shape),
            _resident(band_shape),
        ]

    band_shape = (len(LRU_BAND_START), LRU_KWIN, LRU_NTILE)
    scratch = [pltpu.VMEM((tile, c), F32), pltpu.VMEM((tile, c), F32),
               pltpu.VMEM((tile, c), F32), pltpu.VMEM((1, c), F32)]

    def direction_params(d):
        vec = jnp.stack([conv_b, b_a[d], b_x[d], lam[d]]).astype(F32)
        return vec, _gate_bands(w_a[d]), _gate_bands(w_x[d])

    vec1, wa1, wx1 = direction_params(1)
    hb = pl.pallas_call(
        functools.partial(_lru_bwd_kernel, tile=tile, n_tiles=n_tiles),
        out_shape=jax.ShapeDtypeStruct((b, t, c), F32),
        grid=(b, n_tiles),
        in_specs=scan_specs(True),
        out_specs=pl.BlockSpec((1, tile, c), lambda i, s: (i, tile_idx(s, True), 0)),
        scratch_shapes=scratch,
        compiler_params=_cparams("arbitrary", "arbitrary"),
        name="lru_bwd",
    )(xr, xr, xr, conv_w, vec1, wa1, wx1)

    vec0, wa0, wx0 = direction_params(0)
    tok = lambda n: pl.BlockSpec((1, tile, n), lambda i, s: (i, s, 0))
    return pl.pallas_call(
        functools.partial(_lru_fwd_kernel, tile=tile, n_tiles=n_tiles),
        out_shape=jax.ShapeDtypeStruct((b, t, D_MODEL), F32),
        grid=(b, n_tiles),
        in_specs=scan_specs(False) + [tok(c), tok(c), tok(D_MODEL), _resident((c, D_MODEL))],
        out_specs=tok(D_MODEL),
        scratch_shapes=scratch,
        compiler_params=_cparams("arbitrary", "arbitrary"),
        name="lru_fwd",
    )(xr, xr, xr, conv_w, vec0, wa0, wx0, hb, gate, x, w_out.astype(BF16)).reshape(b * t, D_MODEL)


def _rope_lanes(v, cos, sin):
    return v * cos + pltpu.roll(v, LANE // 2, 1) * sin


def _mla_proj_kernel(x_ref, g_ref, wdq_ref, gq_ref, wuq_ref, wdkv_ref, gkv_ref, wukv_ref, cos_ref, sin_ref,
                     qn_ref, qr_ref, kn_ref, kr_ref, v_ref):
    scale = (MLA_NOPE + MLA_ROPE) ** -0.5 * LOG2E
    nq = MLA_HEADS * MLA_NOPE
    xn = _rms(x_ref[0], g_ref[...]).astype(BF16)
    cos = cos_ref[...]
    sin = sin_ref[...]
    cq = _rms(_dot(xn, wdq_ref[...]), gq_ref[...]).astype(BF16)
    q = _dot(cq, wuq_ref[...])
    qn_ref[0] = (q[:, 0:nq] * scale).astype(BF16)
    for p in range(MLA_HEADS // 2):
        sl = slice(nq + p * LANE, nq + (p + 1) * LANE)
        qr_ref[0, :, p * LANE:(p + 1) * LANE] = (_rope_lanes(q[:, sl], cos, sin) * scale).astype(BF16)
    kva = _dot(xn, wdkv_ref[...])
    kr_ref[0] = _rope_lanes(kva[:, MLA_KV_RANK:MLA_KV_RANK + LANE], cos, sin).astype(BF16)
    ckv = _rms(kva[:, 0:MLA_KV_RANK], gkv_ref[...]).astype(BF16)
    kv = _dot(ckv, wukv_ref[...])
    kn_ref[0] = kv[:, 0:nq].astype(BF16)
    v_ref[0] = kv[:, nq:2 * nq].astype(BF16)


def _mla_attn_kernel(qn_ref, qr_ref, kn_ref, kr_ref, v_ref, o_ref,
                     ql_ref, s_ref, p_ref, mxb_ref, m_ref, l_ref, acc_ref, *, seq):
    tq, tk = MLA_TQ, MLA_TK
    rows = 2 * tq
    n = seq // tk
    assert n >= 2 and n % 2 == 0
    qp = jnp.concatenate([qn_ref[0], qr_ref[0]], axis=1)
    lane = lax.broadcasted_iota(jnp.int32, (1, 2 * LANE), 1)
    in_a = (lane < MLA_NOPE) | ((lane >= LANE) & (lane < LANE + MLA_ROPE))
    in_b = ((lane >= MLA_NOPE) & (lane < LANE)) | ((lane >= LANE + MLA_ROPE) & (lane < LANE + 2 * MLA_ROPE))
    zero = jnp.zeros_like(qp)
    ql_ref[0:tq, :] = jnp.where(in_a, qp, zero)
    ql_ref[tq:rows, :] = jnp.where(in_b, qp, zero)
    m_ref[...] = jnp.full(m_ref.shape, NEG, F32)
    l_ref[...] = jnp.zeros(l_ref.shape, F32)
    acc_ref[...] = jnp.zeros(acc_ref.shape, F32)

    def key_rows(kt):
        return pl.ds(kt * tk if isinstance(kt, int) else pl.multiple_of(kt * tk, tk), tk)

    def qk(kt, slot):
        kk = jnp.concatenate([kn_ref[0, key_rows(kt), :], kr_ref[0, key_rows(kt), :]], axis=1)
        s = _dot_nt(ql_ref[...], kk)
        s_ref[slot] = s
        mxb_ref[slot] = jnp.broadcast_to(jnp.max(s, axis=1, keepdims=True), (rows, LANE))

    def pv(kt, slot):
        return _dot(p_ref[slot], v_ref[0, key_rows(kt), :])

    def stage(kt, slot, qk_next, pv_prev):
        if qk_next:
            qk(kt + 1, 1 - slot)
        o_prev = pv(kt - 1, 1 - slot) if pv_prev else None
        m_old = m_ref[...]
        m_new = jnp.maximum(m_old, mxb_ref[slot])
        alpha = jnp.exp2(m_old - m_new)
        m_ref[...] = m_new
        for c in range(rows // MLA_CH):
            r = slice(c * MLA_CH, (c + 1) * MLA_CH)
            ps = [jnp.exp2(s_ref[slot, r, g * LANE:(g + 1) * LANE] - m_new[r, :]) for g in range(tk // LANE)]
            l_ref[r, :] = alpha[r, :] * l_ref[r, :] + functools.reduce(lambda a, b: a + b, ps)
            p_ref[slot, r, :] = jnp.concatenate(ps, axis=1).astype(BF16)
            if pv_prev:
                acc_ref[r, :] = alpha[r, :] * (acc_ref[r, :] + o_prev[r, :])

    qk(0, 0)
    stage(0, 0, True, False)

    def pair(i, carry):
        kt = 1 + 2 * i
        stage(kt, 1, True, True)
        stage(kt + 1, 0, True, True)
        return carry

    lax.fori_loop(0, (n - 2) // 2, pair, 0)
    stage(n - 1, 1, False, True)
    o = (acc_ref[...] + pv(n - 1, 1)) / jnp.sum(l_ref[...], axis=1, keepdims=True)
    lo = lax.broadcasted_iota(jnp.int32, (1, LANE), 1) < MLA_V
    o_ref[0] = jnp.where(lo, o[:tq], o[tq:]).astype(BF16)


def _mla_weights(w_uq, w_dkv, w_ukv):
    h, nope, rope = MLA_HEADS, MLA_NOPE, MLA_ROPE
    half = rope // 2
    wq = w_uq.reshape(MLA_Q_RANK, h, nope + rope)
    wq_nope = wq[:, :, :nope].reshape(MLA_Q_RANK, h * nope)
    wq_rope = wq[:, :, nope:].reshape(MLA_Q_RANK, h // 2, 2, rope)
    wq_swap = jnp.concatenate([wq_rope[..., half:], wq_rope[..., :half]], axis=-1)
    wq_rope = jnp.concatenate([wq_rope.reshape(MLA_Q_RANK, h // 2, 2 * rope),
                               wq_swap.reshape(MLA_Q_RANK, h // 2, 2 * rope)], axis=-1)
    wuq = jnp.concatenate([wq_nope, wq_rope.reshape(MLA_Q_RANK, h // 2 * LANE)], axis=1)
    w_kr = w_dkv[:, MLA_KV_RANK:]
    w_krs = jnp.concatenate([w_kr[:, half:], w_kr[:, :half]], axis=1)
    wdkv = jnp.concatenate([w_dkv[:, :MLA_KV_RANK], w_kr, w_kr, w_krs, w_krs], axis=1)
    wkv = w_ukv.reshape(MLA_KV_RANK, h, nope + MLA_V)
    wukv = jnp.concatenate([wkv[:, :, :nope].reshape(MLA_KV_RANK, h * nope),
                            wkv[:, :, nope:].reshape(MLA_KV_RANK, h * MLA_V)], axis=1)
    return wuq.astype(BF16), wdkv.astype(BF16), wukv.astype(BF16)


def _rope_tables(t):
    pos = jnp.arange(t, dtype=F32)
    inv = ROPE_THETA ** (-jnp.arange(0, MLA_ROPE, 2, dtype=F32) / MLA_ROPE)
    ang = pos[:, None] * inv[None, :]
    cos, sin = jnp.cos(ang), jnp.sin(ang)
    pad = jnp.zeros((t, LANE // 2), F32)
    return (jnp.concatenate([cos, cos, cos, cos, pad], axis=1),
            jnp.concatenate([-sin, sin, -sin, sin, pad], axis=1))


def _mla_layer(x, g, w_dq, g_q, w_uq, w_dkv, g_kv, w_ukv, w_o):
    b, t, _ = x.shape
    tm = TOK_TILE
    wuq, wdkv, wukv = _mla_weights(w_uq, w_dkv, w_ukv)
    cos, sin = _rope_tables(t)
    wide = MLA_HEADS * MLA_NOPE
    tok = lambda n: pl.BlockSpec((1, tm, n), lambda i, s: (i, s, 0))
    big = jax.ShapeDtypeStruct((b, t, wide), BF16)
    qn, qr, kn, kr, v = pl.pallas_call(
        _mla_proj_kernel,
        out_shape=(big, big, big, jax.ShapeDtypeStruct((b, t, LANE), BF16), big),
        grid=(b, t // tm),
        in_specs=[tok(D_MODEL), _resident((1, D_MODEL)),
                  _resident((D_MODEL, MLA_Q_RANK)), _resident((1, MLA_Q_RANK)), _resident(wuq.shape),
                  _resident(wdkv.shape), _resident((1, MLA_KV_RANK)), _resident(wukv.shape),
                  pl.BlockSpec((tm, LANE), lambda i, s: (s, 0)), pl.BlockSpec((tm, LANE), lambda i, s: (s, 0))],
        out_specs=(tok(wide), tok(wide), tok(wide), tok(LANE), tok(wide)),
        compiler_params=_cparams("parallel", "parallel"),
        name="mla_proj",
    )(x, g.reshape(1, D_MODEL), w_dq.astype(BF16), g_q.reshape(1, MLA_Q_RANK), wuq,
      wdkv, g_kv.reshape(1, MLA_KV_RANK), wukv, cos, sin)

    q_spec = pl.BlockSpec((1, MLA_TQ, LANE), lambda i, p, s: (i, s, p))
    kv_spec = pl.BlockSpec((1, t, LANE), lambda i, p, s: (i, 0, p))
    o = pl.pallas_call(
        functools.partial(_mla_attn_kernel, seq=t),
        out_shape=jax.ShapeDtypeStruct((b, t, MLA_HEADS * MLA_V), BF16),
        grid=(b, MLA_HEADS // 2, t // MLA_TQ),
        in_specs=[q_spec, q_spec, kv_spec,
                  pl.BlockSpec((1, t, LANE), lambda i, p, s: (i, 0, 0)), kv_spec],
        out_specs=q_spec,
        scratch_shapes=[pltpu.VMEM((2 * MLA_TQ, 2 * LANE), BF16),
                        pltpu.VMEM((2, 2 * MLA_TQ, MLA_TK), F32),
                        pltpu.VMEM((2, 2 * MLA_TQ, MLA_TK), BF16),
                        pltpu.VMEM((2, 2 * MLA_TQ, LANE), F32),
                        pltpu.VMEM((2 * MLA_TQ, LANE), F32),
                        pltpu.VMEM((2 * MLA_TQ, LANE), F32),
                        pltpu.VMEM((2 * MLA_TQ, LANE), F32)],
        compiler_params=_cparams("parallel", "parallel", "parallel"),
        name="mla_attn",
    )(qn, qr, kn, kr, v)
    return o.reshape(b * t, MLA_HEADS * MLA_V), w_o.astype(BF16)


def kernel(x_prompt, x_sample, norm_mix, norm_ffn, norm_final, na_w_qkv, na_rpb, na_w_o, lru_w_in, lru_conv_w, lru_conv_b, lru_w_a, lru_b_a, lru_w_x, lru_b_x, lru_lam, lru_w_out, mla_w_dq, mla_g_q, mla_w_uq, mla_w_dkv, mla_g_kv, mla_w_ukv, mla_w_o, ffn_w_gate, ffn_w_up, ffn_w_down):
    wg = ffn_w_gate.astype(BF16)
    wu = ffn_w_up.astype(BF16)
    wd = ffn_w_down.astype(BF16)

    def run(x):
        b, t, _ = x.shape
        x = x.reshape(b * t, D_MODEL)
        for i in range(DEPTH):
            j, m = divmod(i, N_MIXERS)
            x3 = x.reshape(b, t, D_MODEL)
            attn = w_o = None
            if m == 0:
                attn, w_o = _na_layer(x3, norm_mix[i], na_w_qkv[j], na_rpb[j], na_w_o[j])
            elif m == 1:
                x = _lru_layer(x3, norm_mix[i], lru_w_in[j], lru_conv_w[j], lru_conv_b[j], lru_w_a[j], lru_b_a[j],
                               lru_w_x[j], lru_b_x[j], lru_lam[j], lru_w_out[j])
            else:
                attn, w_o = _mla_layer(x3, norm_mix[i], mla_w_dq[j], mla_g_q[j], mla_w_uq[j], mla_w_dkv[j],
                                       mla_g_kv[j], mla_w_ukv[j], mla_w_o[j])
            x = _ffn(x, norm_ffn[i], wg[i], wu[i], wd[i], norm_final if i == DEPTH - 1 else None, attn, w_o)
        return x.reshape(b, t, D_MODEL)

    return run(x_prompt), run(x_sample)
```

```python
import functools

import numpy as np
import jax
import jax.numpy as jnp
from jax import lax
from jax.experimental import pallas as pl
from jax.experimental.pallas import tpu as pltpu

F32 = jnp.float32
BF16 = jnp.bfloat16

D_MODEL = 1024
DEPTH = 4
N_MIXERS = 3
RMS_EPS = 1e-6
D_FF = 2816

GRID_W = 64
NA_HEADS = 16
NA_HEAD_DIM = 64
NA_KH = 8
NA_KW = 16
NA_QBW = 16
NA_KBW = 32
NA_NCB = GRID_W // NA_QBW
NA_BAND_START = tuple(int(v) for v in np.clip(np.arange(NA_NCB) * NA_QBW - NA_KW // 2, 0, GRID_W - NA_KBW))
NA_RB = 8
NA_KROWS = 16
NA_KSUB = 4
NA_NE = 28
NEG = -1e30
LOG2E = float(np.log2(np.e))

LRU_WIDTH = 1408
LRU_BLOCKS = 16
LRU_BLOCK = 88
CONV_W = 4
LRU_C = 8.0
HALO = 8
LRU_NTILE = 256
LRU_KWIN = 512
LRU_BAND_START = tuple(min(max(c * LRU_NTILE - 128, 0), LRU_WIDTH - LRU_KWIN)
                       for c in range(-(-LRU_WIDTH // LRU_NTILE)))

MLA_HEADS = 16
MLA_Q_RANK = 384
MLA_KV_RANK = 256
MLA_NOPE = 64
MLA_ROPE = 32
MLA_V = 64
ROPE_THETA = 10000.0

LANE = 128
TOK_TILE = 512
FF_CHUNK = 256
MLA_TQ = 512
MLA_TK = 512
MLA_CH = 32
MLA_PAIRS = 2
VMEM_LIMIT = 56 * 1024 * 1024


def _cparams(*sem):
    return pltpu.CompilerParams(dimension_semantics=sem, vmem_limit_bytes=VMEM_LIMIT)


def _resident(shape):
    nd = len(shape)
    return pl.BlockSpec(shape, lambda *_: (0,) * nd, pipeline_mode=pl.Buffered(1))


def _rms(x, g):
    return x * lax.rsqrt(jnp.mean(x * x, axis=-1, keepdims=True) + RMS_EPS) * g


def _dot(a, b):
    return jnp.dot(a, b, preferred_element_type=F32)


def _dot_nt(a, b):
    return lax.dot_general(a, b, (((1,), (1,)), ((), ())), preferred_element_type=F32)


def _ffn_kernel(*refs, mixer, final):
    x_ref, g_ref, wg_ref, wu_ref, wd_ref = refs[:5]
    o_ref = refs[-1]
    x = x_ref[...]
    if mixer:
        x = x + _dot(refs[5][...], refs[6][...])
    xn = _rms(x, g_ref[...]).astype(BF16)
    acc = x
    for c in range(D_FF // FF_CHUNK):
        sl = slice(c * FF_CHUNK, (c + 1) * FF_CHUNK)
        g = _dot(xn, wg_ref[:, sl])
        u = _dot(xn, wu_ref[:, sl])
        h = (g * jax.nn.sigmoid(g) * u).astype(BF16)
        acc = acc + _dot(h, wd_ref[sl, :])
    if final:
        acc = _rms(acc, refs[-2][...])
    o_ref[...] = acc


def _ffn(x, g, layer, wg, wu, wd, g_final=None, attn=None, w_o=None):
    m = x.shape[0]
    final = g_final is not None
    mixer = attn is not None
    ins = [x, g.reshape(1, D_MODEL), wg, wu, wd]

    def layer_spec(rows, cols):
        return pl.BlockSpec((None, rows, cols), lambda i: (layer, 0, 0), pipeline_mode=pl.Buffered(1))

    specs = [pl.BlockSpec((TOK_TILE, D_MODEL), lambda i: (i, 0)),
             _resident((1, D_MODEL)),
             layer_spec(D_MODEL, D_FF), layer_spec(D_MODEL, D_FF), layer_spec(D_FF, D_MODEL)]
    if mixer:
        ins += [attn, w_o]
        specs += [pl.BlockSpec((TOK_TILE, attn.shape[1]), lambda i: (i, 0)), _resident(w_o.shape)]
    if final:
        ins.append(g_final.reshape(1, D_MODEL))
        specs.append(_resident((1, D_MODEL)))
    return pl.pallas_call(
        functools.partial(_ffn_kernel, mixer=mixer, final=final),
        out_shape=jax.ShapeDtypeStruct((m, D_MODEL), F32),
        grid=(m // TOK_TILE,),
        in_specs=specs,
        out_specs=pl.BlockSpec((TOK_TILE, D_MODEL), lambda i: (i, 0)),
        compiler_params=_cparams("parallel"),
        name="ffn",
    )(*ins)


def _na_qkv_kernel(x_ref, g_ref, w_ref, q_ref, kb_ref, vb_ref):
    xn = _rms(x_ref[0], g_ref[...]).astype(BF16)
    q = _dot(xn, w_ref[:, 0:D_MODEL])
    q_ref[0] = (q * (NA_HEAD_DIM ** -0.5 * LOG2E)).astype(BF16)
    for part, out_ref in ((1, kb_ref), (2, vb_ref)):
        r = _dot(xn, w_ref[:, part * D_MODEL:(part + 1) * D_MODEL])
        r3 = r.reshape(NA_RB, GRID_W, D_MODEL)
        for j, bs in enumerate(NA_BAND_START):
            out_ref[0, :, j] = r3[:, bs:bs + NA_KBW, :].astype(BF16)


def _na_qkv(x, g, w):
    b, t, _ = x.shape
    rows = t // GRID_W
    tile = NA_RB * GRID_W
    band = jax.ShapeDtypeStruct((b, rows, NA_NCB, NA_KBW, D_MODEL), BF16)
    band_spec = pl.BlockSpec((1, NA_RB, NA_NCB, NA_KBW, D_MODEL), lambda i, r: (i, r, 0, 0, 0))
    return pl.pallas_call(
        _na_qkv_kernel,
        out_shape=(jax.ShapeDtypeStruct((b, t, D_MODEL), BF16), band, band),
        grid=(b, rows // NA_RB),
        in_specs=[pl.BlockSpec((1, tile, D_MODEL), lambda i, r: (i, r, 0)),
                  _resident((1, D_MODEL)),
                  _resident((D_MODEL, 3 * D_MODEL))],
        out_specs=(pl.BlockSpec((1, tile, D_MODEL), lambda i, r: (i, r, 0)), band_spec, band_spec),
        compiler_params=_cparams("parallel", "parallel"),
        name="na_qkv",
    )(x, g.reshape(1, D_MODEL), w)


def _na_key_block0(rb, nkb):
    return jnp.clip(2 * rb - 1, 0, nkb - NA_KROWS // NA_KSUB)


def _na_first_key_row(var, qr):
    half = NA_KH // 2
    return (max(qr - half, 0), qr, min(qr + half, NA_KH))[var]


def _na_attn_body(var, q_ref, k_refs, v_refs, bias_ref, rmask_ref, o_ref):
    lo = lax.broadcasted_iota(jnp.int32, (1, LANE), 1) < NA_HEAD_DIM
    nq = NA_RB * NA_QBW
    ng = NA_KROWS // NA_KSUB
    zero_tile = jnp.zeros((NA_QBW, LANE), BF16)
    for hp in range(NA_HEADS // 2):
        sl = slice(hp * LANE, (hp + 1) * LANE)
        qp = q_ref[0, :, :, sl].reshape(nq, LANE)
        zero = jnp.zeros_like(qp)
        ql = jnp.concatenate([jnp.where(lo, qp, zero), jnp.where(lo, zero, qp)], axis=0)
        kp = jnp.concatenate([r[0, :, 0, :, sl].reshape(NA_KSUB * NA_KBW, LANE) for r in k_refs], axis=0)
        vp = jnp.concatenate([r[0, :, 0, :, sl].reshape(NA_KSUB * NA_KBW, LANE) for r in v_refs], axis=0)
        s = _dot_nt(ql, kp)
        chunks = []
        for a in range(2):
            for qr in range(NA_RB):
                r0 = a * nq + qr * NA_QBW
                rs = _na_first_key_row(var, qr)
                tiles = {}
                for g in range(ng):
                    if NA_KSUB * (g + 1) > rs and NA_KSUB * g < rs + NA_KH:
                        t = (s[r0:r0 + NA_QBW, g * LANE:(g + 1) * LANE]
                             + bias_ref[2 * hp + a, 4 * g - qr + 15 - 4 * var])
                        if not (NA_KSUB * g >= rs and NA_KSUB * (g + 1) <= rs + NA_KH):
                            t = t + rmask_ref[var, qr, g]
                        tiles[g] = t
                chunks.append(tiles)
        lane_max = jnp.concatenate([functools.reduce(jnp.maximum, tiles.values()) for tiles in chunks], axis=0)
        mb = jnp.broadcast_to(jnp.max(lane_max, axis=1, keepdims=True), (2 * nq, LANE))
        ps = []
        for ci, tiles in enumerate(chunks):
            m = mb[ci * NA_QBW:(ci + 1) * NA_QBW, :]
            ps.append(jnp.concatenate(
                [jnp.exp2(tiles[g] - m).astype(BF16) if g in tiles else zero_tile for g in range(ng)], axis=1))
        o2 = _dot(jnp.concatenate(ps, axis=0), jnp.concatenate([vp, jnp.ones_like(vp)], axis=1))
        o = o2[:, 0:LANE] / o2[:, LANE:2 * LANE]
        out = jnp.where(lo, o[:nq], o[nq:]).astype(BF16)
        o_ref[0, :, :, sl] = out.reshape(NA_RB, NA_QBW, LANE)


def _na_attn_kernel(q_ref, k0, k1, k2, k3, v0, v1, v2, v3, bias_ref, rmask_ref, o_ref, *, nkb):
    rb = pl.program_id(2)
    kind = 2 * rb - _na_key_block0(rb, nkb)
    for var in range(3):
        pl.when(kind == var)(functools.partial(
            _na_attn_body, var, q_ref, (k0, k1, k2, k3), (v0, v1, v2, v3), bias_ref, rmask_ref, o_ref))


def _na_tables(rpb):
    n_dr = 2 * NA_KH - 1 + 16
    n_dc = 2 * NA_KW - 1 + 2 * NA_KW
    rpbp = jnp.pad(rpb.astype(F32) * LOG2E, ((0, 0), (8, 8), (NA_KW, NA_KW)))
    src = jnp.stack([rpbp[:, i:i + NA_NE, :] for i in range(NA_KSUB)], axis=2)
    src = src.reshape(NA_HEADS * NA_NE, NA_KSUB * n_dc)
    src = jnp.concatenate([src, jnp.full((NA_HEADS * NA_NE, 1), NEG, F32)], axis=1)
    j = np.arange(NA_NCB)[:, None, None]
    qc = np.arange(NA_QBW)[None, :, None]
    ln = np.arange(LANE)[None, None, :]
    kcol = np.asarray(NA_BAND_START)[j] + ln % NA_KBW
    c = NA_QBW * j + qc
    ws = np.clip(c - NA_KW // 2, 0, GRID_W - NA_KW)
    col_ok = (kcol >= ws) & (kcol < ws + NA_KW)
    pick = np.where(col_ok, (ln // NA_KBW) * n_dc + (kcol - c + NA_KW - 1) + NA_KW, NA_KSUB * n_dc)
    assert n_dr == NA_NE + NA_KSUB - 1 and pick.min() >= 0
    onehot = (jnp.arange(NA_KSUB * n_dc + 1, dtype=jnp.int32)[:, None]
              == jnp.asarray(pick.reshape(1, -1).astype(np.int32))).astype(F32)
    tab = jnp.dot(src, onehot, precision=lax.Precision.HIGHEST)
    tab = tab.reshape(NA_HEADS, NA_NE, NA_NCB, NA_QBW, LANE)

    var = np.arange(3)[:, None, None, None]
    qr = np.arange(NA_RB)[None, :, None, None]
    g = np.arange(NA_KROWS // NA_KSUB)[None, None, :, None]
    kr = NA_KSUB * g + np.arange(LANE)[None, None, None, :] // NA_KBW
    rs = np.where(var == 0, np.maximum(qr - NA_KH // 2, 0),
                  np.where(var == 1, qr, np.minimum(qr + NA_KH // 2, NA_KH)))
    row_ok = (kr >= rs) & (kr < rs + NA_KH)
    rmask = np.where(row_ok, 0.0, NEG).astype(np.float32)
    rmask = np.broadcast_to(rmask[:, :, :, None, :], (3, NA_RB, NA_KROWS // NA_KSUB, NA_QBW, LANE))
    return tab, jnp.asarray(np.ascontiguousarray(rmask))


def _na_attn(q, kb, vb, rpb):
    b, t, _ = q.shape
    rows = t // GRID_W
    assert rows % NA_RB == 0 and rows >= NA_KROWS
    nkb = rows // NA_KSUB
    q4 = q.reshape(b, rows, GRID_W, D_MODEL)
    tab, rmask = _na_tables(rpb)
    q_spec = pl.BlockSpec((1, NA_RB, NA_QBW, D_MODEL), lambda j, i, r: (i, r, j, 0))

    def band_spec(off):
        return pl.BlockSpec((1, NA_KSUB, 1, NA_KBW, D_MODEL),
                            lambda j, i, r: (i, _na_key_block0(r, nkb) + off, j, 0, 0))

    o = pl.pallas_call(
        functools.partial(_na_attn_kernel, nkb=nkb),
        out_shape=jax.ShapeDtypeStruct((b, rows, GRID_W, D_MODEL), BF16),
        grid=(NA_NCB, b, rows // NA_RB),
        in_specs=[q_spec] + [band_spec(off) for off in range(4)] * 2 + [
            pl.BlockSpec((NA_HEADS, NA_NE, None, NA_QBW, LANE), lambda j, i, r: (0, 0, j, 0, 0)),
            _resident(rmask.shape)],
        out_specs=q_spec,
        compiler_params=_cparams("parallel", "parallel", "parallel"),
        name="na_attn",
    )(q4, kb, kb, kb, kb, vb, vb, vb, vb, tab, rmask)
    return o.reshape(b, t, D_MODEL)


def _na_layer(x, g, w_qkv, rpb, w_o):
    b, t, _ = x.shape
    q, kb, vb = _na_qkv(x, g, w_qkv.astype(BF16))
    return _na_attn(q, kb, vb, rpb).reshape(b * t, D_MODEL), w_o.astype(BF16)


def _lru_in_kernel(x_ref, g_ref, w_ref, gate_ref, xr_ref):
    xn = _rms(x_ref[...], g_ref[...]).astype(BF16)
    gate_ref[...] = _dot(xn, w_ref[:, 0:LRU_WIDTH])
    xr_ref[...] = _dot(xn, w_ref[:, LRU_WIDTH:2 * LRU_WIDTH])


def _lru_in(x, g, w):
    m = x.shape[0]
    out = jax.ShapeDtypeStruct((m, LRU_WIDTH), F32)
    spec = pl.BlockSpec((TOK_TILE, LRU_WIDTH), lambda i: (i, 0))
    return pl.pallas_call(
        _lru_in_kernel,
        out_shape=(out, out),
        grid=(m // TOK_TILE,),
        in_specs=[pl.BlockSpec((TOK_TILE, D_MODEL), lambda i: (i, 0)),
                  _resident((1, D_MODEL)),
                  _resident((D_MODEL, 2 * LRU_WIDTH))],
        out_specs=(spec, spec),
        compiler_params=_cparams("parallel"),
        name="lru_in",
    )(x, g.reshape(1, D_MODEL), w)


def _lru_scan_tile(xr_ref, prev_ref, next_ref, cw_ref, vec_ref, wa_ref, wx_ref,
                   a_ref, u_ref, h_ref, carry_ref, *, tile, n_tiles, reverse):
    step = pl.program_id(1)
    ti = (n_tiles - 1 - step) if reverse else step
    tt = tile

    @pl.when(step == 0)
    def _():
        carry_ref[...] = jnp.zeros_like(carry_ref)

    xr = xr_ref[0]
    n_ext = tt + 2 * HALO
    ext = jnp.concatenate([jnp.where(ti > 0, prev_ref[0], 0.0), xr,
                           jnp.where(ti < n_tiles - 1, next_ref[0], 0.0)], axis=0)
    xc = vec_ref[0:1, :] + cw_ref[2:3, :] * xr
    for k in (0, 1, 3):
        xc = xc + cw_ref[k:k + 1, :] * pltpu.roll(ext, (2 - k) % n_ext, 0)[HALO:HALO + tt, :]
    xcb = xc.astype(BF16)
    z = -vec_ref[3:4, :]
    c_pos = 0.5 * LRU_C * (jnp.maximum(z, 0.0) + jnp.log1p(jnp.exp(-jnp.abs(z))))
    c_exp = c_pos * (-LOG2E)
    for c, k0 in enumerate(LRU_BAND_START):
        w = min(LRU_NTILE, LRU_WIDTH - c * LRU_NTILE)
        cols = slice(c * LRU_NTILE, c * LRU_NTILE + w)
        xw = xcb[:, k0:k0 + LRU_KWIN]
        t_a = jnp.tanh(_dot(xw, wa_ref[c, :, 0:w]) + 0.5 * vec_ref[1:2, cols])
        t_x = jnp.tanh(_dot(xw, wx_ref[c, :, 0:w]) + 0.5 * vec_ref[2:3, cols])
        t1 = t_a + 1.0
        a = jnp.exp2(t1 * c_exp[:, cols])
        th = jnp.tanh(t1 * c_pos[:, cols])
        v = th * (a * a + 1.0)
        u = jnp.where(v > 0.0, v * lax.rsqrt(v), 0.0) * ((t_x + 1.0) * (0.5 * xc[:, cols]))
        a_ref[:, cols] = a
        u_ref[:, cols] = u

    def body(k, h):
        t = (tt - 1 - k) if reverse else k
        h = a_ref[pl.ds(t, 1), :] * h + u_ref[pl.ds(t, 1), :]
        h_ref[pl.ds(t, 1), :] = h
        return h

    carry_ref[...] = lax.fori_loop(0, tt, body, carry_ref[...], unroll=8)


def _lru_bwd_kernel(xr_ref, prev_ref, next_ref, cw_ref, vec_ref, wa_ref, wx_ref, hb_ref,
                    a_ref, u_ref, h_ref, carry_ref, *, tile, n_tiles):
    _lru_scan_tile(xr_ref, prev_ref, next_ref, cw_ref, vec_ref, wa_ref, wx_ref,
                   a_ref, u_ref, h_ref, carry_ref, tile=tile, n_tiles=n_tiles, reverse=True)
    hb_ref[0] = h_ref[...]


def _lru_fwd_kernel(xr_ref, prev_ref, next_ref, cw_ref, vec_ref, wa_ref, wx_ref, hb_ref, gate_ref, res_ref, wo_ref,
                    o_ref, a_ref, u_ref, h_ref, carry_ref, *, tile, n_tiles):
    _lru_scan_tile(xr_ref, prev_ref, next_ref, cw_ref, vec_ref, wa_ref, wx_ref,
                   a_ref, u_ref, h_ref, carry_ref, tile=tile, n_tiles=n_tiles, reverse=False)
    y = jax.nn.gelu(gate_ref[0], approximate=True) * (h_ref[...] + hb_ref[0])
    o_ref[0] = res_ref[0] + _dot(y.astype(BF16), wo_ref[...])


def _gate_bands(w):
    half = (0.5 * w).astype(BF16)
    bands = jnp.zeros((len(LRU_BAND_START), LRU_KWIN, LRU_NTILE), BF16)
    for c, k0 in enumerate(LRU_BAND_START):
        lo, hi = c * LRU_NTILE, min((c + 1) * LRU_NTILE, LRU_WIDTH)
        for n in range(lo // LRU_BLOCK, (hi - 1) // LRU_BLOCK + 1):
            r0 = n * LRU_BLOCK - k0
            assert 0 <= r0 and r0 + LRU_BLOCK <= LRU_KWIN
            c0, c1 = max(n * LRU_BLOCK, lo), min((n + 1) * LRU_BLOCK, hi)
            bands = bands.at[c, r0:r0 + LRU_BLOCK, c0 - lo:c1 - lo].set(
                half[n, :, c0 - n * LRU_BLOCK:c1 - n * LRU_BLOCK])
    return bands


def _lru_layer(x, g, w_in, conv_w, conv_b, w_a, b_a, w_x, b_x, lam, w_out):
    b, t, _ = x.shape
    c = LRU_WIDTH
    tile = TOK_TILE
    n_tiles = t // tile
    hpt = tile // HALO
    gate, xr = _lru_in(x.reshape(b * t, D_MODEL), g, w_in.astype(BF16))
    gate = gate.reshape(b, t, c)
    xr = xr.reshape(b, t, c)

    def tile_idx(s, reverse):
        return (n_tiles - 1 - s) if reverse else s

    def scan_specs(reverse):
        return [
            pl.BlockSpec((1, tile, c), lambda i, s: (i, tile_idx(s, reverse), 0)),
            pl.BlockSpec((1, HALO, c), lambda i, s: (i, jnp.maximum(tile_idx(s, reverse) * hpt - 1, 0), 0)),
            pl.BlockSpec((1, HALO, c),
                         lambda i, s: (i, jnp.minimum((tile_idx(s, reverse) + 1) * hpt, t // HALO - 1), 0)),
            _resident((CONV_W, c)),
            _resident((4, c)),
            _resident(band_shape),
            _resident(band_shape),
        ]

    band_shape = (len(LRU_BAND_START), LRU_KWIN, LRU_NTILE)
    scratch = [pltpu.VMEM((tile, c), F32), pltpu.VMEM((tile, c), F32),
               pltpu.VMEM((tile, c), F32), pltpu.VMEM((1, c), F32)]

    def direction_params(d):
        vec = jnp.stack([conv_b, b_a[d], b_x[d], lam[d]]).astype(F32)
        return vec, _gate_bands(w_a[d]), _gate_bands(w_x[d])

    vec1, wa1, wx1 = direction_params(1)
    hb = pl.pallas_call(
        functools.partial(_lru_bwd_kernel, tile=tile, n_tiles=n_tiles),
        out_shape=jax.ShapeDtypeStruct((b, t, c), F32),
        grid=(b, n_tiles),
        in_specs=scan_specs(True),
        out_specs=pl.BlockSpec((1, tile, c), lambda i, s: (i, tile_idx(s, True), 0)),
        scratch_shapes=scratch,
        compiler_params=_cparams("arbitrary", "arbitrary"),
        name="lru_bwd",
    )(xr, xr, xr, conv_w, vec1, wa1, wx1)

    vec0, wa0, wx0 = direction_params(0)
    tok = lambda n: pl.BlockSpec((1, tile, n), lambda i, s: (i, s, 0))
    return pl.pallas_call(
        functools.partial(_lru_fwd_kernel, tile=tile, n_tiles=n_tiles),
        out_shape=jax.ShapeDtypeStruct((b, t, D_MODEL), F32),
        grid=(b, n_tiles),
        in_specs=scan_specs(False) + [tok(c), tok(c), tok(D_MODEL), _resident((c, D_MODEL))],
        out_specs=tok(D_MODEL),
        scratch_shapes=scratch,
        compiler_params=_cparams("arbitrary", "arbitrary"),
        name="lru_fwd",
    )(xr, xr, xr, conv_w, vec0, wa0, wx0, hb, gate, x, w_out.astype(BF16)).reshape(b * t, D_MODEL)


def _rope_lanes(v, cos, sin):
    return v * cos + pltpu.roll(v, LANE // 2, 1) * sin


def _mla_proj_kernel(x_ref, g_ref, wdq_ref, gq_ref, wuq_ref, wdkv_ref, gkv_ref, wukv_ref, cos_ref, sin_ref,
                     qn_ref, qr_ref, kn_ref, kr_ref, v_ref):
    scale = (MLA_NOPE + MLA_ROPE) ** -0.5 * LOG2E
    nq = MLA_HEADS * MLA_NOPE
    xn = _rms(x_ref[0], g_ref[...]).astype(BF16)
    cos = cos_ref[...]
    sin = sin_ref[...]
    cq = _rms(_dot(xn, wdq_ref[...]), gq_ref[...]).astype(BF16)
    q = _dot(cq, wuq_ref[...])
    qn_ref[0] = (q[:, 0:nq] * scale).astype(BF16)
    for p in range(MLA_HEADS // 2):
        sl = slice(nq + p * LANE, nq + (p + 1) * LANE)
        qr_ref[0, :, p * LANE:(p + 1) * LANE] = (_rope_lanes(q[:, sl], cos, sin) * scale).astype(BF16)
    kva = _dot(xn, wdkv_ref[...])
    kr_ref[0] = _rope_lanes(kva[:, MLA_KV_RANK:MLA_KV_RANK + LANE], cos, sin).astype(BF16)
    ckv = _rms(kva[:, 0:MLA_KV_RANK], gkv_ref[...]).astype(BF16)
    kv = _dot(ckv, wukv_ref[...])
    kn_ref[0] = kv[:, 0:nq].astype(BF16)
    v_ref[0] = kv[:, nq:2 * nq].astype(BF16)


def _mla_attn_kernel(qn_ref, qr_ref, kn_ref, kr_ref, v_ref, o_ref,
                     ql_ref, s_ref, p_ref, mxb_ref, m_ref, l_ref, acc_ref, *, seq):
    tq, tk = MLA_TQ, MLA_TK
    rows = 2 * tq
    n = seq // tk
    assert n >= 2 and n % 2 == 0
    lane = lax.broadcasted_iota(jnp.int32, (1, 2 * LANE), 1)
    in_a = (lane < MLA_NOPE) | ((lane >= LANE) & (lane < LANE + MLA_ROPE))
    in_b = ((lane >= MLA_NOPE) & (lane < LANE)) | ((lane >= LANE + MLA_ROPE) & (lane < LANE + 2 * MLA_ROPE))
    lo = lax.broadcasted_iota(jnp.int32, (1, LANE), 1) < MLA_V

    def lanes(pp):
        return slice(pp * LANE, (pp + 1) * LANE)

    for pp in range(MLA_PAIRS):
        qp = jnp.concatenate([qn_ref[0, :, lanes(pp)], qr_ref[0, :, lanes(pp)]], axis=1)
        zero = jnp.zeros_like(qp)
        ql_ref[pp, 0:tq, :] = jnp.where(in_a, qp, zero)
        ql_ref[pp, tq:rows, :] = jnp.where(in_b, qp, zero)

    def key_rows(kt):
        return pl.ds(kt * tk if isinstance(kt, int) else pl.multiple_of(kt * tk, tk), tk)

    def qk(pp, kt, slot):
        kk = jnp.concatenate([kn_ref[0, key_rows(kt), lanes(pp)], kr_ref[0, key_rows(kt), :]], axis=1)
        s = _dot_nt(ql_ref[pp], kk)
        s_ref[slot] = s
        mxb_ref[slot] = jnp.broadcast_to(jnp.max(s, axis=1, keepdims=True), (rows, LANE))

    def pv(pp, kt, slot):
        return _dot(p_ref[slot], v_ref[0, key_rows(kt), lanes(pp)])

    def finish(pp, o_last):
        o = (acc_ref[pp] + o_last) / jnp.sum(l_ref[pp], axis=1, keepdims=True)
        o_ref[0, :, lanes(pp)] = jnp.where(lo, o[:tq], o[tq:]).astype(BF16)

    def stage(pp, kt, slot, nxt):
        first = isinstance(kt, int) and kt == 0
        if nxt is not None:
            qk(nxt[0], nxt[1], 1 - slot)
        o_prev = None
        if not first:
            o_prev = pv(pp, kt - 1, 1 - slot)
        elif pp > 0:
            finish(pp - 1, pv(pp - 1, n - 1, 1 - slot))
        if first:
            m_new = mxb_ref[slot]
        else:
            m_old = m_ref[pp]
            m_new = jnp.maximum(m_old, mxb_ref[slot])
            alpha = jnp.exp2(m_old - m_new)
        m_ref[pp] = m_new
        for c in range(rows // MLA_CH):
            r = slice(c * MLA_CH, (c + 1) * MLA_CH)
            ps = [jnp.exp2(s_ref[slot, r, g * LANE:(g + 1) * LANE] - m_new[r, :]) for g in range(tk // LANE)]
            p_ref[slot, r, :] = jnp.concatenate(ps, axis=1).astype(BF16)
            psum = functools.reduce(lambda a, b: a + b, ps)
            if first:
                l_ref[pp, r, :] = psum
                acc_ref[pp, r, :] = jnp.zeros((MLA_CH, LANE), F32)
            else:
                l_ref[pp, r, :] = alpha[r, :] * l_ref[pp, r, :] + psum
                acc_ref[pp, r, :] = alpha[r, :] * (acc_ref[pp, r, :] + o_prev[r, :])

    tiles = [(pp, kt) for pp in range(MLA_PAIRS) for kt in range(n)]
    qk(0, 0, 0)
    for i, (pp, kt) in enumerate(tiles):
        stage(pp, kt, kt % 2, tiles[i + 1] if i + 1 < len(tiles) else None)
    finish(MLA_PAIRS - 1, pv(MLA_PAIRS - 1, n - 1, (n - 1) % 2))


def _mla_weights(w_uq, w_dkv, w_ukv):
    h, nope, rope = MLA_HEADS, MLA_NOPE, MLA_ROPE
    half = rope // 2
    wq = w_uq.reshape(MLA_Q_RANK, h, nope + rope)
    wq_nope = wq[:, :, :nope].reshape(MLA_Q_RANK, h * nope)
    wq_rope = wq[:, :, nope:].reshape(MLA_Q_RANK, h // 2, 2, rope)
    wq_swap = jnp.concatenate([wq_rope[..., half:], wq_rope[..., :half]], axis=-1)
    wq_rope = jnp.concatenate([wq_rope.reshape(MLA_Q_RANK, h // 2, 2 * rope),
                               wq_swap.reshape(MLA_Q_RANK, h // 2, 2 * rope)], axis=-1)
    wuq = jnp.concatenate([wq_nope, wq_rope.reshape(MLA_Q_RANK, h // 2 * LANE)], axis=1)
    w_kr = w_dkv[:, MLA_KV_RANK:]
    w_krs = jnp.concatenate([w_kr[:, half:], w_kr[:, :half]], axis=1)
    wdkv = jnp.concatenate([w_dkv[:, :MLA_KV_RANK], w_kr, w_kr, w_krs, w_krs], axis=1)
    wkv = w_ukv.reshape(MLA_KV_RANK, h, nope + MLA_V)
    wukv = jnp.concatenate([wkv[:, :, :nope].reshape(MLA_KV_RANK, h * nope),
                            wkv[:, :, nope:].reshape(MLA_KV_RANK, h * MLA_V)], axis=1)
    return wuq.astype(BF16), wdkv.astype(BF16), wukv.astype(BF16)


def _rope_tables(t):
    pos = jnp.arange(t, dtype=F32)
    inv = ROPE_THETA ** (-jnp.arange(0, MLA_ROPE, 2, dtype=F32) / MLA_ROPE)
    ang = pos[:, None] * inv[None, :]
    cos, sin = jnp.cos(ang), jnp.sin(ang)
    pad = jnp.zeros((t, LANE // 2), F32)
    return (jnp.concatenate([cos, cos, cos, cos, pad], axis=1),
            jnp.concatenate([-sin, sin, -sin, sin, pad], axis=1))


def _mla_layer(x, g, w_dq, g_q, w_uq, w_dkv, g_kv, w_ukv, w_o):
    b, t, _ = x.shape
    tm = TOK_TILE
    wuq, wdkv, wukv = _mla_weights(w_uq, w_dkv, w_ukv)
    cos, sin = _rope_tables(t)
    wide = MLA_HEADS * MLA_NOPE
    tok = lambda n: pl.BlockSpec((1, tm, n), lambda i, s: (i, s, 0))
    big = jax.ShapeDtypeStruct((b, t, wide), BF16)
    qn, qr, kn, kr, v = pl.pallas_call(
        _mla_proj_kernel,
        out_shape=(big, big, big, jax.ShapeDtypeStruct((b, t, LANE), BF16), big),
        grid=(b, t // tm),
        in_specs=[tok(D_MODEL), _resident((1, D_MODEL)),
                  _resident((D_MODEL, MLA_Q_RANK)), _resident((1, MLA_Q_RANK)), _resident(wuq.shape),
                  _resident(wdkv.shape), _resident((1, MLA_KV_RANK)), _resident(wukv.shape),
                  pl.BlockSpec((tm, LANE), lambda i, s: (s, 0)), pl.BlockSpec((tm, LANE), lambda i, s: (s, 0))],
        out_specs=(tok(wide), tok(wide), tok(wide), tok(LANE), tok(wide)),
        compiler_params=_cparams("parallel", "parallel"),
        name="mla_proj",
    )(x, g.reshape(1, D_MODEL), w_dq.astype(BF16), g_q.reshape(1, MLA_Q_RANK), wuq,
      wdkv, g_kv.reshape(1, MLA_KV_RANK), wukv, cos, sin)

    width = MLA_PAIRS * LANE
    rows = 2 * MLA_TQ
    q_spec = pl.BlockSpec((1, MLA_TQ, width), lambda i, p, s: (i, s, p))
    kv_spec = pl.BlockSpec((1, t, width), lambda i, p, s: (i, 0, p))
    o = pl.pallas_call(
        functools.partial(_mla_attn_kernel, seq=t),
        out_shape=jax.ShapeDtypeStruct((b, t, MLA_HEADS * MLA_V), BF16),
        grid=(b, MLA_HEADS // 2 // MLA_PAIRS, t // MLA_TQ),
        in_specs=[q_spec, q_spec, kv_spec,
                  pl.BlockSpec((1, t, LANE), lambda i, p, s: (i, 0, 0)), kv_spec],
        out_specs=q_spec,
        scratch_shapes=[pltpu.VMEM((MLA_PAIRS, rows, 2 * LANE), BF16),
                        pltpu.VMEM((2, rows, MLA_TK), F32),
                        pltpu.VMEM((2, rows, MLA_TK), BF16),
                        pltpu.VMEM((2, rows, LANE), F32),
                        pltpu.VMEM((MLA_PAIRS, rows, LANE), F32),
                        pltpu.VMEM((MLA_PAIRS, rows, LANE), F32),
                        pltpu.VMEM((MLA_PAIRS, rows, LANE), F32)],
        compiler_params=_cparams("parallel", "parallel", "parallel"),
        name="mla_attn",
    )(qn, qr, kn, kr, v)
    return o.reshape(b * t, MLA_HEADS * MLA_V), w_o.astype(BF16)


def kernel(x_prompt, x_sample, norm_mix, norm_ffn, norm_final, na_w_qkv, na_rpb, na_w_o, lru_w_in, lru_conv_w, lru_conv_b, lru_w_a, lru_b_a, lru_w_x, lru_b_x, lru_lam, lru_w_out, mla_w_dq, mla_g_q, mla_w_uq, mla_w_dkv, mla_g_kv, mla_w_ukv, mla_w_o, ffn_w_gate, ffn_w_up, ffn_w_down):
    wg = ffn_w_gate.astype(BF16)
    wu = ffn_w_up.astype(BF16)
    wd = ffn_w_down.astype(BF16)

    def run(x):
        b, t, _ = x.shape
        x = x.reshape(b * t, D_MODEL)
        for i in range(DEPTH):
            j, m = divmod(i, N_MIXERS)
            x3 = x.reshape(b, t, D_MODEL)
            attn = w_o = None
            if m == 0:
                attn, w_o = _na_layer(x3, norm_mix[i], na_w_qkv[j], na_rpb[j], na_w_o[j])
            elif m == 1:
                x = _lru_layer(x3, norm_mix[i], lru_w_in[j], lru_conv_w[j], lru_conv_b[j], lru_w_a[j], lru_b_a[j],
                               lru_w_x[j], lru_b_x[j], lru_lam[j], lru_w_out[j])
            else:
                attn, w_o = _mla_layer(x3, norm_mix[i], mla_w_dq[j], mla_g_q[j], mla_w_uq[j], mla_w_dkv[j],
                                       mla_g_kv[j], mla_w_ukv[j], mla_w_o[j])
            x = _ffn(x, norm_ffn[i], i, wg, wu, wd, norm_final if i == DEPTH - 1 else None, attn, w_o)
        return x.reshape(b, t, D_MODEL)

    return run(x_prompt), run(x_sample)
```

```python
import functools

import numpy as np
import jax
import jax.numpy as jnp
from jax import lax
from jax.experimental import pallas as pl
from jax.experimental.pallas import tpu as pltpu

F32 = jnp.float32
BF16 = jnp.bfloat16

D_MODEL = 1024
DEPTH = 4
N_MIXERS = 3
RMS_EPS = 1e-6
D_FF = 2816

GRID_W = 64
NA_HEADS = 16
NA_HEAD_DIM = 64
NA_KH = 8
NA_KW = 16
NA_QBW = 16
NA_KBW = 32
NA_NCB = GRID_W // NA_QBW
NA_BAND_START = tuple(int(v) for v in np.clip(np.arange(NA_NCB) * NA_QBW - NA_KW // 2, 0, GRID_W - NA_KBW))
NA_RB = 8
NA_KROWS = 16
NA_KSUB = 4
NA_NE = 28
NEG = -1e30
LOG2E = float(np.log2(np.e))

LRU_WIDTH = 1408
LRU_BLOCKS = 16
LRU_BLOCK = 88
CONV_W = 4
LRU_C = 8.0
HALO = 8
LRU_NTILE = 256
LRU_KWIN = 512
LRU_BAND_START = tuple(min(max(c * LRU_NTILE - 128, 0), LRU_WIDTH - LRU_KWIN)
                       for c in range(-(-LRU_WIDTH // LRU_NTILE)))

MLA_HEADS = 16
MLA_Q_RANK = 384
MLA_KV_RANK = 256
MLA_NOPE = 64
MLA_ROPE = 32
MLA_V = 64
ROPE_THETA = 10000.0

LANE = 128
TOK_TILE = 512
FF_CHUNK = 256
MLA_TQ = 512
MLA_TK = 512
MLA_CH = 32
MLA_PAIRS = 2
VMEM_LIMIT = 56 * 1024 * 1024


def _cparams(*sem):
    return pltpu.CompilerParams(dimension_semantics=sem, vmem_limit_bytes=VMEM_LIMIT)


def _resident(shape):
    nd = len(shape)
    return pl.BlockSpec(shape, lambda *_: (0,) * nd, pipeline_mode=pl.Buffered(1))


def _rms(x, g):
    return x * lax.rsqrt(jnp.mean(x * x, axis=-1, keepdims=True) + RMS_EPS) * g


def _dot(a, b):
    return jnp.dot(a, b, preferred_element_type=F32)


def _dot_nt(a, b):
    return lax.dot_general(a, b, (((1,), (1,)), ((), ())), preferred_element_type=F32)


def _ffn_kernel(*refs, mixer, final):
    x_ref, g_ref, wg_ref, wu_ref, wd_ref = refs[:5]
    o_ref = refs[-1]
    x = x_ref[...]
    if mixer:
        x = x + _dot(refs[5][...], refs[6][...])
    xn = _rms(x, g_ref[...]).astype(BF16)
    acc = x
    for c in range(D_FF // FF_CHUNK):
        sl = slice(c * FF_CHUNK, (c + 1) * FF_CHUNK)
        g = _dot(xn, wg_ref[:, sl])
        u = _dot(xn, wu_ref[:, sl])
        h = (g * jax.nn.sigmoid(g) * u).astype(BF16)
        acc = acc + _dot(h, wd_ref[sl, :])
    if final:
        acc = _rms(acc, refs[-2][...])
    o_ref[...] = acc


def _ffn(x, g, layer, wg, wu, wd, g_final=None, attn=None, w_o=None):
    m = x.shape[0]
    final = g_final is not None
    mixer = attn is not None
    ins = [x, g.reshape(1, D_MODEL), wg, wu, wd]

    def layer_spec(rows, cols):
        return pl.BlockSpec((None, rows, cols), lambda i: (layer, 0, 0), pipeline_mode=pl.Buffered(1))

    specs = [pl.BlockSpec((TOK_TILE, D_MODEL), lambda i: (i, 0)),
             _resident((1, D_MODEL)),
             layer_spec(D_MODEL, D_FF), layer_spec(D_MODEL, D_FF), layer_spec(D_FF, D_MODEL)]
    if mixer:
        ins += [attn, w_o]
        specs += [pl.BlockSpec((TOK_TILE, attn.shape[1]), lambda i: (i, 0)), _resident(w_o.shape)]
    if final:
        ins.append(g_final.reshape(1, D_MODEL))
        specs.append(_resident((1, D_MODEL)))
    return pl.pallas_call(
        functools.partial(_ffn_kernel, mixer=mixer, final=final),
        out_shape=jax.ShapeDtypeStruct((m, D_MODEL), F32),
        grid=(m // TOK_TILE,),
        in_specs=specs,
        out_specs=pl.BlockSpec((TOK_TILE, D_MODEL), lambda i: (i, 0)),
        compiler_params=_cparams("parallel"),
        name="ffn",
    )(*ins)


def _na_qkv_kernel(x_ref, g_ref, w_ref, q_ref, kb_ref, vb_ref):
    xn = _rms(x_ref[0], g_ref[...]).astype(BF16)
    q = _dot(xn, w_ref[:, 0:D_MODEL])
    q_ref[0] = (q * (NA_HEAD_DIM ** -0.5 * LOG2E)).astype(BF16)
    for part, out_ref in ((1, kb_ref), (2, vb_ref)):
        r = _dot(xn, w_ref[:, part * D_MODEL:(part + 1) * D_MODEL])
        r3 = r.reshape(NA_RB, GRID_W, D_MODEL)
        for j, bs in enumerate(NA_BAND_START):
            out_ref[0, :, j] = r3[:, bs:bs + NA_KBW, :].astype(BF16)


def _na_qkv(x, g, w):
    b, t, _ = x.shape
    rows = t // GRID_W
    tile = NA_RB * GRID_W
    band = jax.ShapeDtypeStruct((b, rows, NA_NCB, NA_KBW, D_MODEL), BF16)
    band_spec = pl.BlockSpec((1, NA_RB, NA_NCB, NA_KBW, D_MODEL), lambda i, r: (i, r, 0, 0, 0))
    return pl.pallas_call(
        _na_qkv_kernel,
        out_shape=(jax.ShapeDtypeStruct((b, t, D_MODEL), BF16), band, band),
        grid=(b, rows // NA_RB),
        in_specs=[pl.BlockSpec((1, tile, D_MODEL), lambda i, r: (i, r, 0)),
                  _resident((1, D_MODEL)),
                  _resident((D_MODEL, 3 * D_MODEL))],
        out_specs=(pl.BlockSpec((1, tile, D_MODEL), lambda i, r: (i, r, 0)), band_spec, band_spec),
        compiler_params=_cparams("parallel", "parallel"),
        name="na_qkv",
    )(x, g.reshape(1, D_MODEL), w)


def _na_key_block0(rb, nkb):
    return jnp.clip(2 * rb - 1, 0, nkb - NA_KROWS // NA_KSUB)


def _na_first_key_row(var, qr):
    half = NA_KH // 2
    return (max(qr - half, 0), qr, min(qr + half, NA_KH))[var]


def _na_attn_body(var, q_ref, k_refs, v_refs, bias_ref, rmask_ref, o_ref):
    lo = lax.broadcasted_iota(jnp.int32, (1, LANE), 1) < NA_HEAD_DIM
    nq = NA_RB * NA_QBW
    ng = NA_KROWS // NA_KSUB
    zero_tile = jnp.zeros((NA_QBW, LANE), BF16)
    for hp in range(NA_HEADS // 2):
        sl = slice(hp * LANE, (hp + 1) * LANE)
        qp = q_ref[0, :, :, sl].reshape(nq, LANE)
        zero = jnp.zeros_like(qp)
        ql = jnp.concatenate([jnp.where(lo, qp, zero), jnp.where(lo, zero, qp)], axis=0)
        kp = jnp.concatenate([r[0, :, 0, :, sl].reshape(NA_KSUB * NA_KBW, LANE) for r in k_refs], axis=0)
        vp = jnp.concatenate([r[0, :, 0, :, sl].reshape(NA_KSUB * NA_KBW, LANE) for r in v_refs], axis=0)
        s = _dot_nt(ql, kp)
        chunks = []
        for a in range(2):
            for qr in range(NA_RB):
                r0 = a * nq + qr * NA_QBW
                rs = _na_first_key_row(var, qr)
                tiles = {}
                for g in range(ng):
                    if NA_KSUB * (g + 1) > rs and NA_KSUB * g < rs + NA_KH:
                        t = (s[r0:r0 + NA_QBW, g * LANE:(g + 1) * LANE]
                             + bias_ref[2 * hp + a, 4 * g - qr + 15 - 4 * var])
                        if not (NA_KSUB * g >= rs and NA_KSUB * (g + 1) <= rs + NA_KH):
                            t = t + rmask_ref[var, qr, g]
                        tiles[g] = t
                chunks.append(tiles)
        lane_max = jnp.concatenate([functools.reduce(jnp.maximum, tiles.values()) for tiles in chunks], axis=0)
        mb = jnp.broadcast_to(jnp.max(lane_max, axis=1, keepdims=True), (2 * nq, LANE))
        ps = []
        for ci, tiles in enumerate(chunks):
            m = mb[ci * NA_QBW:(ci + 1) * NA_QBW, :]
            ps.append(jnp.concatenate(
                [jnp.exp2(tiles[g] - m).astype(BF16) if g in tiles else zero_tile for g in range(ng)], axis=1))
        o2 = _dot(jnp.concatenate(ps, axis=0), jnp.concatenate([vp, jnp.ones_like(vp)], axis=1))
        o = o2[:, 0:LANE] / o2[:, LANE:2 * LANE]
        out = jnp.where(lo, o[:nq], o[nq:]).astype(BF16)
        o_ref[0, :, :, sl] = out.reshape(NA_RB, NA_QBW, LANE)


def _na_attn_kernel(q_ref, k0, k1, k2, k3, v0, v1, v2, v3, bias_ref, rmask_ref, o_ref, *, nkb):
    rb = pl.program_id(2)
    kind = 2 * rb - _na_key_block0(rb, nkb)
    for var in range(3):
        pl.when(kind == var)(functools.partial(
            _na_attn_body, var, q_ref, (k0, k1, k2, k3), (v0, v1, v2, v3), bias_ref, rmask_ref, o_ref))


def _na_tables(rpb):
    n_dr = 2 * NA_KH - 1 + 16
    n_dc = 2 * NA_KW - 1 + 2 * NA_KW
    rpbp = jnp.pad(rpb.astype(F32) * LOG2E, ((0, 0), (8, 8), (NA_KW, NA_KW)))
    src = jnp.stack([rpbp[:, i:i + NA_NE, :] for i in range(NA_KSUB)], axis=2)
    src = src.reshape(NA_HEADS * NA_NE, NA_KSUB * n_dc)
    src = jnp.concatenate([src, jnp.full((NA_HEADS * NA_NE, 1), NEG, F32)], axis=1)
    j = np.arange(NA_NCB)[:, None, None]
    qc = np.arange(NA_QBW)[None, :, None]
    ln = np.arange(LANE)[None, None, :]
    kcol = np.asarray(NA_BAND_START)[j] + ln % NA_KBW
    c = NA_QBW * j + qc
    ws = np.clip(c - NA_KW // 2, 0, GRID_W - NA_KW)
    col_ok = (kcol >= ws) & (kcol < ws + NA_KW)
    pick = np.where(col_ok, (ln // NA_KBW) * n_dc + (kcol - c + NA_KW - 1) + NA_KW, NA_KSUB * n_dc)
    assert n_dr == NA_NE + NA_KSUB - 1 and pick.min() >= 0
    onehot = (jnp.arange(NA_KSUB * n_dc + 1, dtype=jnp.int32)[:, None]
              == jnp.asarray(pick.reshape(1, -1).astype(np.int32))).astype(F32)
    tab = jnp.dot(src, onehot, precision=lax.Precision.HIGHEST)
    tab = tab.reshape(NA_HEADS, NA_NE, NA_NCB, NA_QBW, LANE)

    var = np.arange(3)[:, None, None, None]
    qr = np.arange(NA_RB)[None, :, None, None]
    g = np.arange(NA_KROWS // NA_KSUB)[None, None, :, None]
    kr = NA_KSUB * g + np.arange(LANE)[None, None, None, :] // NA_KBW
    rs = np.where(var == 0, np.maximum(qr - NA_KH // 2, 0),
                  np.where(var == 1, qr, np.minimum(qr + NA_KH // 2, NA_KH)))
    row_ok = (kr >= rs) & (kr < rs + NA_KH)
    rmask = np.where(row_ok, 0.0, NEG).astype(np.float32)
    rmask = np.broadcast_to(rmask[:, :, :, None, :], (3, NA_RB, NA_KROWS // NA_KSUB, NA_QBW, LANE))
    return tab, jnp.asarray(np.ascontiguousarray(rmask))


def _na_attn(q, kb, vb, rpb):
    b, t, _ = q.shape
    rows = t // GRID_W
    assert rows % NA_RB == 0 and rows >= NA_KROWS
    nkb = rows // NA_KSUB
    q4 = q.reshape(b, rows, GRID_W, D_MODEL)
    tab, rmask = _na_tables(rpb)
    q_spec = pl.BlockSpec((1, NA_RB, NA_QBW, D_MODEL), lambda j, i, r: (i, r, j, 0))

    def band_spec(off):
        return pl.BlockSpec((1, NA_KSUB, 1, NA_KBW, D_MODEL),
                            lambda j, i, r: (i, _na_key_block0(r, nkb) + off, j, 0, 0))

    o = pl.pallas_call(
        functools.partial(_na_attn_kernel, nkb=nkb),
        out_shape=jax.ShapeDtypeStruct((b, rows, GRID_W, D_MODEL), BF16),
        grid=(NA_NCB, b, rows // NA_RB),
        in_specs=[q_spec] + [band_spec(off) for off in range(4)] * 2 + [
            pl.BlockSpec((NA_HEADS, NA_NE, None, NA_QBW, LANE), lambda j, i, r: (0, 0, j, 0, 0)),
            _resident(rmask.shape)],
        out_specs=q_spec,
        compiler_params=_cparams("parallel", "parallel", "parallel"),
        name="na_attn",
    )(q4, kb, kb, kb, kb, vb, vb, vb, vb, tab, rmask)
    return o.reshape(b, t, D_MODEL)


def _na_layer(x, g, w_qkv, rpb, w_o):
    b, t, _ = x.shape
    q, kb, vb = _na_qkv(x, g, w_qkv.astype(BF16))
    return _na_attn(q, kb, vb, rpb).reshape(b * t, D_MODEL), w_o.astype(BF16)


def _lru_in_kernel(x_ref, prev_ref, next_ref, g_ref, w_ref, cw_ref, cb_ref, gate_ref, xc_ref, *, tiles_per_seq):
    ti = pl.program_id(0) % tiles_per_seq
    tt = TOK_TILE
    n_ext = tt + 2 * HALO
    xe = jnp.concatenate([jnp.where(ti > 0, prev_ref[...], 0.0), x_ref[...],
                          jnp.where(ti < tiles_per_seq - 1, next_ref[...], 0.0)], axis=0)
    xn = _rms(xe, g_ref[...]).astype(BF16)
    for c0 in range(0, LRU_WIDTH, LRU_NTILE):
        cols = slice(c0, min(c0 + LRU_NTILE, LRU_WIDTH))
        gate_ref[:, cols] = jax.nn.gelu(_dot(xn[HALO:HALO + tt, :], w_ref[:, cols]), approximate=True)
        ext = _dot(xn, w_ref[:, LRU_WIDTH + cols.start:LRU_WIDTH + cols.stop])
        xc = cb_ref[:, cols] + cw_ref[2:3, cols] * ext[HALO:HALO + tt, :]
        for k in (0, 1, 3):
            xc = xc + cw_ref[k:k + 1, cols] * pltpu.roll(ext, (2 - k) % n_ext, 0)[HALO:HALO + tt, :]
        xc_ref[:, cols] = xc


def _lru_in(x, g, w, conv_w, conv_b, t):
    m = x.shape[0]
    hpt = TOK_TILE // HALO
    out = jax.ShapeDtypeStruct((m, LRU_WIDTH), F32)
    spec = pl.BlockSpec((TOK_TILE, LRU_WIDTH), lambda i: (i, 0))
    return pl.pallas_call(
        functools.partial(_lru_in_kernel, tiles_per_seq=t // TOK_TILE),
        out_shape=(out, out),
        grid=(m // TOK_TILE,),
        in_specs=[pl.BlockSpec((TOK_TILE, D_MODEL), lambda i: (i, 0)),
                  pl.BlockSpec((HALO, D_MODEL), lambda i: (jnp.maximum(i * hpt - 1, 0), 0)),
                  pl.BlockSpec((HALO, D_MODEL), lambda i: (jnp.minimum((i + 1) * hpt, m // HALO - 1), 0)),
                  _resident((1, D_MODEL)),
                  _resident((D_MODEL, 2 * LRU_WIDTH)),
                  _resident((CONV_W, LRU_WIDTH)),
                  _resident((1, LRU_WIDTH))],
        out_specs=(spec, spec),
        compiler_params=_cparams("parallel"),
        name="lru_in",
    )(x, x, x, g.reshape(1, D_MODEL), w, conv_w, conv_b.reshape(1, LRU_WIDTH))


def _lru_scan_tile(xc_ref, vec_ref, wa_ref, wx_ref,
                   a_ref, u_ref, h_ref, carry_ref, *, tile, reverse):
    step = pl.program_id(1)
    tt = tile
    n_groups = a_ref.shape[0]

    @pl.when(step == 0)
    def _():
        carry_ref[...] = jnp.zeros_like(carry_ref)
        a_ref[n_groups - 1] = jnp.zeros(a_ref.shape[1:], F32)
        u_ref[n_groups - 1] = jnp.zeros(u_ref.shape[1:], F32)

    xc = xc_ref[0]
    xcb = xc.astype(BF16)
    z = -vec_ref[2:3, :]
    c_pos = 0.5 * LRU_C * (jnp.maximum(z, 0.0) + jnp.log1p(jnp.exp(-jnp.abs(z))))
    c_exp = c_pos * (-LOG2E)
    for c, k0 in enumerate(LRU_BAND_START):
        w = min(LRU_NTILE, LRU_WIDTH - c * LRU_NTILE)
        cols = slice(c * LRU_NTILE, c * LRU_NTILE + w)
        xw = xcb[:, k0:k0 + LRU_KWIN]
        t_a = jnp.tanh(_dot(xw, wa_ref[c, :, 0:w]) + 0.5 * vec_ref[0:1, cols])
        t_x = jnp.tanh(_dot(xw, wx_ref[c, :, 0:w]) + 0.5 * vec_ref[1:2, cols])
        t1 = t_a + 1.0
        a = jnp.exp2(t1 * c_exp[:, cols])
        th = jnp.tanh(t1 * c_pos[:, cols])
        v = th * (a * a + 1.0)
        u = jnp.where(v > 0.0, v * lax.rsqrt(v), 0.0) * ((t_x + 1.0) * (0.5 * xc[:, cols]))
        for j in range(w // LANE):
            g, s = divmod(c * (LRU_NTILE // LANE) + j, 8)
            a_ref[g, pl.ds(s, tt, stride=8), :] = a[:, j * LANE:(j + 1) * LANE]
            u_ref[g, pl.ds(s, tt, stride=8), :] = u[:, j * LANE:(j + 1) * LANE]

    def body(k, hs):
        pair = (tt // 2 - 1 - k) if reverse else k
        rows = pl.ds(pl.multiple_of(pair * 16, 16), 16)
        out = []
        for g in range(n_groups):
            a2, u2 = a_ref[g, rows, :], u_ref[g, rows, :]
            first, second = (slice(8, 16), slice(0, 8)) if reverse else (slice(0, 8), slice(8, 16))
            a0, u0, a1, u1 = a2[first], u2[first], a2[second], u2[second]
            h0 = a0 * hs[g] + u0
            h1 = (a1 * a0) * hs[g] + (a1 * u0 + u1)
            h_ref[g, rows, :] = jnp.concatenate([h1, h0] if reverse else [h0, h1], axis=0)
            out.append(h1)
        return tuple(out)

    hs = lax.fori_loop(0, tt // 2, body, tuple(carry_ref[g] for g in range(n_groups)), unroll=4)
    for g in range(n_groups):
        carry_ref[g] = hs[g]
    return jnp.concatenate([h_ref[k // 8, pl.ds(k % 8, tt, stride=8), :] for k in range(LRU_WIDTH // LANE)], axis=1)


def _lru_bwd_kernel(xc_ref, vec_ref, wa_ref, wx_ref, hb_ref, a_ref, u_ref, h_ref, carry_ref, *, tile):
    hb_ref[0] = _lru_scan_tile(xc_ref, vec_ref, wa_ref, wx_ref, a_ref, u_ref, h_ref, carry_ref,
                               tile=tile, reverse=True)


def _lru_fwd_kernel(xc_ref, vec_ref, wa_ref, wx_ref, hb_ref, gate_ref, res_ref, wo_ref,
                    o_ref, a_ref, u_ref, h_ref, carry_ref, *, tile):
    h = _lru_scan_tile(xc_ref, vec_ref, wa_ref, wx_ref, a_ref, u_ref, h_ref, carry_ref, tile=tile, reverse=False)
    y = gate_ref[0] * (h + hb_ref[0])
    o_ref[0] = res_ref[0] + _dot(y.astype(BF16), wo_ref[...])


def _gate_bands(w):
    half = (0.5 * w).astype(BF16)
    bands = jnp.zeros((len(LRU_BAND_START), LRU_KWIN, LRU_NTILE), BF16)
    for c, k0 in enumerate(LRU_BAND_START):
        lo, hi = c * LRU_NTILE, min((c + 1) * LRU_NTILE, LRU_WIDTH)
        for n in range(lo // LRU_BLOCK, (hi - 1) // LRU_BLOCK + 1):
            r0 = n * LRU_BLOCK - k0
            assert 0 <= r0 and r0 + LRU_BLOCK <= LRU_KWIN
            c0, c1 = max(n * LRU_BLOCK, lo), min((n + 1) * LRU_BLOCK, hi)
            bands = bands.at[c, r0:r0 + LRU_BLOCK, c0 - lo:c1 - lo].set(
                half[n, :, c0 - n * LRU_BLOCK:c1 - n * LRU_BLOCK])
    return bands


def _lru_layer(x, g, w_in, conv_w, conv_b, w_a, b_a, w_x, b_x, lam, w_out):
    b, t, _ = x.shape
    c = LRU_WIDTH
    tile = TOK_TILE
    n_tiles = t // tile
    gate, xc = _lru_in(x.reshape(b * t, D_MODEL), g, w_in.astype(BF16), conv_w, conv_b, t)
    gate = gate.reshape(b, t, c)
    xc = xc.reshape(b, t, c)
    band_shape = (len(LRU_BAND_START), LRU_KWIN, LRU_NTILE)

    def scan_specs(reverse):
        return [pl.BlockSpec((1, tile, c), lambda i, s: (i, (n_tiles - 1 - s) if reverse else s, 0)),
                _resident((3, c)), _resident(band_shape), _resident(band_shape)]

    n_groups = -(-(c // LANE) // 8)
    slab_rows = pltpu.VMEM((n_groups, tile * 8, LANE), F32)
    scratch = [slab_rows, slab_rows, slab_rows, pltpu.VMEM((n_groups, 8, LANE), F32)]

    def direction_params(d):
        vec = jnp.stack([b_a[d], b_x[d], lam[d]]).astype(F32)
        return vec, _gate_bands(w_a[d]), _gate_bands(w_x[d])

    hb = pl.pallas_call(
        functools.partial(_lru_bwd_kernel, tile=tile),
        out_shape=jax.ShapeDtypeStruct((b, t, c), F32),
        grid=(b, n_tiles),
        in_specs=scan_specs(True),
        out_specs=pl.BlockSpec((1, tile, c), lambda i, s: (i, n_tiles - 1 - s, 0)),
        scratch_shapes=scratch,
        compiler_params=_cparams("arbitrary", "arbitrary"),
        name="lru_bwd",
    )(xc, *direction_params(1))

    tok = lambda n: pl.BlockSpec((1, tile, n), lambda i, s: (i, s, 0))
    return pl.pallas_call(
        functools.partial(_lru_fwd_kernel, tile=tile),
        out_shape=jax.ShapeDtypeStruct((b, t, D_MODEL), F32),
        grid=(b, n_tiles),
        in_specs=scan_specs(False) + [tok(c), tok(c), tok(D_MODEL), _resident((c, D_MODEL))],
        out_specs=tok(D_MODEL),
        scratch_shapes=scratch,
        compiler_params=_cparams("arbitrary", "arbitrary"),
        name="lru_fwd",
    )(xc, *direction_params(0), hb, gate, x, w_out.astype(BF16)).reshape(b * t, D_MODEL)


def _rope_lanes(v, cos, sin):
    return v * cos + pltpu.roll(v, LANE // 2, 1) * sin


def _mla_proj_kernel(x_ref, g_ref, wdq_ref, gq_ref, wuq_ref, wdkv_ref, gkv_ref, wukv_ref, cos_ref, sin_ref,
                     qn_ref, qr_ref, kn_ref, kr_ref, v_ref):
    scale = (MLA_NOPE + MLA_ROPE) ** -0.5 * LOG2E
    nq = MLA_HEADS * MLA_NOPE
    xn = _rms(x_ref[0], g_ref[...]).astype(BF16)
    cos = cos_ref[...]
    sin = sin_ref[...]
    cq = _rms(_dot(xn, wdq_ref[...]), gq_ref[...]).astype(BF16)
    q = _dot(cq, wuq_ref[...])
    qn_ref[0] = (q[:, 0:nq] * scale).astype(BF16)
    for p in range(MLA_HEADS // 2):
        sl = slice(nq + p * LANE, nq + (p + 1) * LANE)
        qr_ref[0, :, p * LANE:(p + 1) * LANE] = (_rope_lanes(q[:, sl], cos, sin) * scale).astype(BF16)
    kva = _dot(xn, wdkv_ref[...])
    kr_ref[0] = _rope_lanes(kva[:, MLA_KV_RANK:MLA_KV_RANK + LANE], cos, sin).astype(BF16)
    ckv = _rms(kva[:, 0:MLA_KV_RANK], gkv_ref[...]).astype(BF16)
    kv = _dot(ckv, wukv_ref[...])
    kn_ref[0] = kv[:, 0:nq].astype(BF16)
    v_ref[0] = kv[:, nq:2 * nq].astype(BF16)


def _mla_attn_kernel(qn_ref, qr_ref, kn_ref, kr_ref, v_ref, o_ref,
                     ql_ref, s_ref, p_ref, mxb_ref, m_ref, l_ref, acc_ref, *, seq):
    tq, tk = MLA_TQ, MLA_TK
    rows = 2 * tq
    n = seq // tk
    assert n >= 2 and n % 2 == 0
    lane = lax.broadcasted_iota(jnp.int32, (1, 2 * LANE), 1)
    in_a = (lane < MLA_NOPE) | ((lane >= LANE) & (lane < LANE + MLA_ROPE))
    in_b = ((lane >= MLA_NOPE) & (lane < LANE)) | ((lane >= LANE + MLA_ROPE) & (lane < LANE + 2 * MLA_ROPE))
    lo = lax.broadcasted_iota(jnp.int32, (1, LANE), 1) < MLA_V

    def lanes(pp):
        return slice(pp * LANE, (pp + 1) * LANE)

    for pp in range(MLA_PAIRS):
        qp = jnp.concatenate([qn_ref[0, :, lanes(pp)], qr_ref[0, :, lanes(pp)]], axis=1)
        zero = jnp.zeros_like(qp)
        ql_ref[pp, 0:tq, :] = jnp.where(in_a, qp, zero)
        ql_ref[pp, tq:rows, :] = jnp.where(in_b, qp, zero)

    def key_rows(kt):
        return pl.ds(kt * tk if isinstance(kt, int) else pl.multiple_of(kt * tk, tk), tk)

    def qk(pp, kt, slot):
        kk = jnp.concatenate([kn_ref[0, key_rows(kt), lanes(pp)], kr_ref[0, key_rows(kt), :]], axis=1)
        s = _dot_nt(ql_ref[pp], kk)
        s_ref[slot] = s
        mxb_ref[slot] = jnp.broadcast_to(jnp.max(s, axis=1, keepdims=True), (rows, LANE))

    def pv(pp, kt, slot):
        return _dot(p_ref[slot], v_ref[0, key_rows(kt), lanes(pp)])

    def finish(pp, o_last):
        o = (acc_ref[pp] + o_last) / jnp.sum(l_ref[pp], axis=1, keepdims=True)
        o_ref[0, :, lanes(pp)] = jnp.where(lo, o[:tq], o[tq:]).astype(BF16)

    def stage(pp, kt, slot, nxt):
        first = isinstance(kt, int) and kt == 0
        if nxt is not None:
            qk(nxt[0], nxt[1], 1 - slot)
        o_prev = None
        if not first:
            o_prev = pv(pp, kt - 1, 1 - slot)
        elif pp > 0:
            finish(pp - 1, pv(pp - 1, n - 1, 1 - slot))
        if first:
            m_new = mxb_ref[slot]
        else:
            m_old = m_ref[pp]
            m_new = jnp.maximum(m_old, mxb_ref[slot])
            alpha = jnp.exp2(m_old - m_new)
        m_ref[pp] = m_new
        for c in range(rows // MLA_CH):
            r = slice(c * MLA_CH, (c + 1) * MLA_CH)
            ps = [jnp.exp2(s_ref[slot, r, g * LANE:(g + 1) * LANE] - m_new[r, :]) for g in range(tk // LANE)]
            p_ref[slot, r, :] = jnp.concatenate(ps, axis=1).astype(BF16)
            psum = functools.reduce(lambda a, b: a + b, ps)
            if first:
                l_ref[pp, r, :] = psum
                acc_ref[pp, r, :] = jnp.zeros((MLA_CH, LANE), F32)
            else:
                l_ref[pp, r, :] = alpha[r, :] * l_ref[pp, r, :] + psum
                acc_ref[pp, r, :] = alpha[r, :] * (acc_ref[pp, r, :] + o_prev[r, :])

    tiles = [(pp, kt) for pp in range(MLA_PAIRS) for kt in range(n)]
    qk(0, 0, 0)
    for i, (pp, kt) in enumerate(tiles):
        stage(pp, kt, kt % 2, tiles[i + 1] if i + 1 < len(tiles) else None)
    finish(MLA_PAIRS - 1, pv(MLA_PAIRS - 1, n - 1, (n - 1) % 2))


def _mla_weights(w_uq, w_dkv, w_ukv):
    h, nope, rope = MLA_HEADS, MLA_NOPE, MLA_ROPE
    half = rope // 2
    wq = w_uq.reshape(MLA_Q_RANK, h, nope + rope)
    wq_nope = wq[:, :, :nope].reshape(MLA_Q_RANK, h * nope)
    wq_rope = wq[:, :, nope:].reshape(MLA_Q_RANK, h // 2, 2, rope)
    wq_swap = jnp.concatenate([wq_rope[..., half:], wq_rope[..., :half]], axis=-1)
    wq_rope = jnp.concatenate([wq_rope.reshape(MLA_Q_RANK, h // 2, 2 * rope),
                               wq_swap.reshape(MLA_Q_RANK, h // 2, 2 * rope)], axis=-1)
    wuq = jnp.concatenate([wq_nope, wq_rope.reshape(MLA_Q_RANK, h // 2 * LANE)], axis=1)
    w_kr = w_dkv[:, MLA_KV_RANK:]
    w_krs = jnp.concatenate([w_kr[:, half:], w_kr[:, :half]], axis=1)
    wdkv = jnp.concatenate([w_dkv[:, :MLA_KV_RANK], w_kr, w_kr, w_krs, w_krs], axis=1)
    wkv = w_ukv.reshape(MLA_KV_RANK, h, nope + MLA_V)
    wukv = jnp.concatenate([wkv[:, :, :nope].reshape(MLA_KV_RANK, h * nope),
                            wkv[:, :, nope:].reshape(MLA_KV_RANK, h * MLA_V)], axis=1)
    return wuq.astype(BF16), wdkv.astype(BF16), wukv.astype(BF16)


def _rope_tables(t):
    pos = jnp.arange(t, dtype=F32)
    inv = ROPE_THETA ** (-jnp.arange(0, MLA_ROPE, 2, dtype=F32) / MLA_ROPE)
    ang = pos[:, None] * inv[None, :]
    cos, sin = jnp.cos(ang), jnp.sin(ang)
    pad = jnp.zeros((t, LANE // 2), F32)
    return (jnp.concatenate([cos, cos, cos, cos, pad], axis=1),
            jnp.concatenate([-sin, sin, -sin, sin, pad], axis=1))


def _mla_layer(x, g, w_dq, g_q, w_uq, w_dkv, g_kv, w_ukv, w_o):
    b, t, _ = x.shape
    tm = TOK_TILE
    wuq, wdkv, wukv = _mla_weights(w_uq, w_dkv, w_ukv)
    cos, sin = _rope_tables(t)
    wide = MLA_HEADS * MLA_NOPE
    tok = lambda n: pl.BlockSpec((1, tm, n), lambda i, s: (i, s, 0))
    big = jax.ShapeDtypeStruct((b, t, wide), BF16)
    qn, qr, kn, kr, v = pl.pallas_call(
        _mla_proj_kernel,
        out_shape=(big, big, big, jax.ShapeDtypeStruct((b, t, LANE), BF16), big),
        grid=(b, t // tm),
        in_specs=[tok(D_MODEL), _resident((1, D_MODEL)),
                  _resident((D_MODEL, MLA_Q_RANK)), _resident((1, MLA_Q_RANK)), _resident(wuq.shape),
                  _resident(wdkv.shape), _resident((1, MLA_KV_RANK)), _resident(wukv.shape),
                  pl.BlockSpec((tm, LANE), lambda i, s: (s, 0)), pl.BlockSpec((tm, LANE), lambda i, s: (s, 0))],
        out_specs=(tok(wide), tok(wide), tok(wide), tok(LANE), tok(wide)),
        compiler_params=_cparams("parallel", "parallel"),
        name="mla_proj",
    )(x, g.reshape(1, D_MODEL), w_dq.astype(BF16), g_q.reshape(1, MLA_Q_RANK), wuq,
      wdkv, g_kv.reshape(1, MLA_KV_RANK), wukv, cos, sin)

    width = MLA_PAIRS * LANE
    rows = 2 * MLA_TQ
    q_spec = pl.BlockSpec((1, MLA_TQ, width), lambda i, p, s: (i, s, p))
    kv_spec = pl.BlockSpec((1, t, width), lambda i, p, s: (i, 0, p))
    o = pl.pallas_call(
        functools.partial(_mla_attn_kernel, seq=t),
        out_shape=jax.ShapeDtypeStruct((b, t, MLA_HEADS * MLA_V), BF16),
        grid=(b, MLA_HEADS // 2 // MLA_PAIRS, t // MLA_TQ),
        in_specs=[q_spec, q_spec, kv_spec,
                  pl.BlockSpec((1, t, LANE), lambda i, p, s: (i, 0, 0)), kv_spec],
        out_specs=q_spec,
        scratch_shapes=[pltpu.VMEM((MLA_PAIRS, rows, 2 * LANE), BF16),
                        pltpu.VMEM((2, rows, MLA_TK), F32),
                        pltpu.VMEM((2, rows, MLA_TK), BF16),
                        pltpu.VMEM((2, rows, LANE), F32),
                        pltpu.VMEM((MLA_PAIRS, rows, LANE), F32),
                        pltpu.VMEM((MLA_PAIRS, rows, LANE), F32),
                        pltpu.VMEM((MLA_PAIRS, rows, LANE), F32)],
        compiler_params=_cparams("parallel", "parallel", "parallel"),
        name="mla_attn",
    )(qn, qr, kn, kr, v)
    return o.reshape(b * t, MLA_HEADS * MLA_V), w_o.astype(BF16)


def kernel(x_prompt, x_sample, norm_mix, norm_ffn, norm_final, na_w_qkv, na_rpb, na_w_o, lru_w_in, lru_conv_w, lru_conv_b, lru_w_a, lru_b_a, lru_w_x, lru_b_x, lru_lam, lru_w_out, mla_w_dq, mla_g_q, mla_w_uq, mla_w_dkv, mla_g_kv, mla_w_ukv, mla_w_o, ffn_w_gate, ffn_w_up, ffn_w_down):
    wg = ffn_w_gate.astype(BF16)
    wu = ffn_w_up.astype(BF16)
    wd = ffn_w_down.astype(BF16)

    def run(x):
        b, t, _ = x.shape
        x = x.reshape(b * t, D_MODEL)
        for i in range(DEPTH):
            j, m = divmod(i, N_MIXERS)
            x3 = x.reshape(b, t, D_MODEL)
            attn = w_o = None
            if m == 0:
                attn, w_o = _na_layer(x3, norm_mix[i], na_w_qkv[j], na_rpb[j], na_w_o[j])
            elif m == 1:
                x = _lru_layer(x3, norm_mix[i], lru_w_in[j], lru_conv_w[j], lru_conv_b[j], lru_w_a[j], lru_b_a[j],
                               lru_w_x[j], lru_b_x[j], lru_lam[j], lru_w_out[j])
            else:
                attn, w_o = _mla_layer(x3, norm_mix[i], mla_w_dq[j], mla_g_q[j], mla_w_uq[j], mla_w_dkv[j],
                                       mla_g_kv[j], mla_w_ukv[j], mla_w_o[j])
            x = _ffn(x, norm_ffn[i], i, wg, wu, wd, norm_final if i == DEPTH - 1 else None, attn, w_o)
        return x.reshape(b, t, D_MODEL)

    return run(x_prompt), run(x_sample)
```

```python
import functools

import numpy as np
import jax
import jax.numpy as jnp
from jax import lax
from jax.experimental import pallas as pl
from jax.experimental.pallas import tpu as pltpu

F32 = jnp.float32
BF16 = jnp.bfloat16

D_MODEL = 1024
DEPTH = 4
N_MIXERS = 3
RMS_EPS = 1e-6
D_FF = 2816

GRID_W = 64
NA_HEADS = 16
NA_HEAD_DIM = 64
NA_KH = 8
NA_KW = 16
NA_QBW = 16
NA_KBW = 32
NA_NCB = GRID_W // NA_QBW
NA_BAND_START = tuple(int(v) for v in np.clip(np.arange(NA_NCB) * NA_QBW - NA_KW // 2, 0, GRID_W - NA_KBW))
NA_RB = 8
NA_KROWS = 16
NA_KSUB = 4
NA_NE = 28
NEG = -1e30
LOG2E = float(np.log2(np.e))

LRU_WIDTH = 1408
LRU_BLOCKS = 16
LRU_BLOCK = 88
CONV_W = 4
LRU_C = 8.0
HALO = 8
LRU_NTILE = 256
LRU_KWIN = 512
LRU_BAND_START = tuple(min(max(c * LRU_NTILE - 128, 0), LRU_WIDTH - LRU_KWIN)
                       for c in range(-(-LRU_WIDTH // LRU_NTILE)))

MLA_HEADS = 16
MLA_Q_RANK = 384
MLA_KV_RANK = 256
MLA_NOPE = 64
MLA_ROPE = 32
MLA_V = 64
ROPE_THETA = 10000.0

LANE = 128
TOK_TILE = 512
FFN_TILE = 1024
FF_CHUNK = 256
MLA_TQ = 512
MLA_TK = 512
MLA_CH = 32
MLA_PAIRS = 2
VMEM_LIMIT = 56 * 1024 * 1024


def _cparams(*sem):
    return pltpu.CompilerParams(dimension_semantics=sem, vmem_limit_bytes=VMEM_LIMIT)


def _resident(shape):
    nd = len(shape)
    return pl.BlockSpec(shape, lambda *_: (0,) * nd, pipeline_mode=pl.Buffered(1))


def _rms(x, g):
    return x * lax.rsqrt(jnp.mean(x * x, axis=-1, keepdims=True) + RMS_EPS) * g


def _dot(a, b):
    return jnp.dot(a, b, preferred_element_type=F32)


def _dot_nt(a, b):
    return lax.dot_general(a, b, (((1,), (1,)), ((), ())), preferred_element_type=F32)


def _ffn_kernel(*refs, mixer, final):
    x_ref, g_ref, wg_ref, wu_ref, wd_ref = refs[:5]
    o_ref = refs[-1]
    x = x_ref[...]
    if mixer:
        x = x + _dot(refs[5][...], refs[6][...])
    xn = _rms(x, g_ref[...]).astype(BF16)
    acc = x
    for c in range(D_FF // FF_CHUNK):
        sl = slice(c * FF_CHUNK, (c + 1) * FF_CHUNK)
        g = _dot(xn, wg_ref[:, sl])
        u = _dot(xn, wu_ref[:, sl])
        h = (g * jax.nn.sigmoid(g) * u).astype(BF16)
        acc = acc + _dot(h, wd_ref[sl, :])
    if final:
        acc = _rms(acc, refs[-2][...])
    o_ref[...] = acc


def _ffn(x, g, layer, wg, wu, wd, g_final=None, attn=None, w_o=None):
    m = x.shape[0]
    final = g_final is not None
    mixer = attn is not None
    ins = [x, g.reshape(1, D_MODEL), wg, wu, wd]

    def layer_spec(rows, cols):
        return pl.BlockSpec((None, rows, cols), lambda i: (layer, 0, 0), pipeline_mode=pl.Buffered(1))

    specs = [pl.BlockSpec((FFN_TILE, D_MODEL), lambda i: (i, 0)),
             _resident((1, D_MODEL)),
             layer_spec(D_MODEL, D_FF), layer_spec(D_MODEL, D_FF), layer_spec(D_FF, D_MODEL)]
    if mixer:
        ins += [attn, w_o]
        specs += [pl.BlockSpec((FFN_TILE, attn.shape[1]), lambda i: (i, 0)), _resident(w_o.shape)]
    if final:
        ins.append(g_final.reshape(1, D_MODEL))
        specs.append(_resident((1, D_MODEL)))
    return pl.pallas_call(
        functools.partial(_ffn_kernel, mixer=mixer, final=final),
        out_shape=jax.ShapeDtypeStruct((m, D_MODEL), F32),
        grid=(m // FFN_TILE,),
        in_specs=specs,
        out_specs=pl.BlockSpec((FFN_TILE, D_MODEL), lambda i: (i, 0)),
        compiler_params=_cparams("parallel"),
        name="ffn",
    )(*ins)


def _na_qkv_kernel(x_ref, g_ref, w_ref, q_ref, kb_ref, vb_ref):
    xn = _rms(x_ref[0], g_ref[...]).astype(BF16)
    q = _dot(xn, w_ref[:, 0:D_MODEL])
    q_ref[0] = (q * (NA_HEAD_DIM ** -0.5 * LOG2E)).astype(BF16)
    for part, out_ref in ((1, kb_ref), (2, vb_ref)):
        r = _dot(xn, w_ref[:, part * D_MODEL:(part + 1) * D_MODEL])
        r3 = r.reshape(NA_RB, GRID_W, D_MODEL)
        for j, bs in enumerate(NA_BAND_START):
            out_ref[0, :, j] = r3[:, bs:bs + NA_KBW, :].astype(BF16)


def _na_qkv(x, g, w):
    b, t, _ = x.shape
    rows = t // GRID_W
    tile = NA_RB * GRID_W
    band = jax.ShapeDtypeStruct((b, rows, NA_NCB, NA_KBW, D_MODEL), BF16)
    band_spec = pl.BlockSpec((1, NA_RB, NA_NCB, NA_KBW, D_MODEL), lambda i, r: (i, r, 0, 0, 0))
    return pl.pallas_call(
        _na_qkv_kernel,
        out_shape=(jax.ShapeDtypeStruct((b, t, D_MODEL), BF16), band, band),
        grid=(b, rows // NA_RB),
        in_specs=[pl.BlockSpec((1, tile, D_MODEL), lambda i, r: (i, r, 0)),
                  _resident((1, D_MODEL)),
                  _resident((D_MODEL, 3 * D_MODEL))],
        out_specs=(pl.BlockSpec((1, tile, D_MODEL), lambda i, r: (i, r, 0)), band_spec, band_spec),
        compiler_params=_cparams("parallel", "parallel"),
        name="na_qkv",
    )(x, g.reshape(1, D_MODEL), w)


def _na_key_block0(rb, nkb):
    return jnp.clip(2 * rb - 1, 0, nkb - NA_KROWS // NA_KSUB)


def _na_first_key_row(var, qr):
    half = NA_KH // 2
    return (max(qr - half, 0), qr, min(qr + half, NA_KH))[var]


def _na_attn_body(var, q_ref, k_refs, v_refs, bias_ref, rmask_ref, o_ref):
    lo = lax.broadcasted_iota(jnp.int32, (1, LANE), 1) < NA_HEAD_DIM
    nq = NA_RB * NA_QBW
    ng = NA_KROWS // NA_KSUB
    zero_tile = jnp.zeros((NA_QBW, LANE), BF16)
    for hp in range(NA_HEADS // 2):
        sl = slice(hp * LANE, (hp + 1) * LANE)
        qp = q_ref[0, :, :, sl].reshape(nq, LANE)
        zero = jnp.zeros_like(qp)
        ql = jnp.concatenate([jnp.where(lo, qp, zero), jnp.where(lo, zero, qp)], axis=0)
        kp = jnp.concatenate([r[0, :, 0, :, sl].reshape(NA_KSUB * NA_KBW, LANE) for r in k_refs], axis=0)
        vp = jnp.concatenate([r[0, :, 0, :, sl].reshape(NA_KSUB * NA_KBW, LANE) for r in v_refs], axis=0)
        s = _dot_nt(ql, kp)
        chunks = []
        for a in range(2):
            for qr in range(NA_RB):
                r0 = a * nq + qr * NA_QBW
                rs = _na_first_key_row(var, qr)
                tiles = {}
                for g in range(ng):
                    if NA_KSUB * (g + 1) > rs and NA_KSUB * g < rs + NA_KH:
                        t = (s[r0:r0 + NA_QBW, g * LANE:(g + 1) * LANE]
                             + bias_ref[2 * hp + a, 4 * g - qr + 15 - 4 * var])
                        if not (NA_KSUB * g >= rs and NA_KSUB * (g + 1) <= rs + NA_KH):
                            t = t + rmask_ref[var, qr, g]
                        tiles[g] = t
                chunks.append(tiles)
        lane_max = jnp.concatenate([functools.reduce(jnp.maximum, tiles.values()) for tiles in chunks], axis=0)
        mb = jnp.broadcast_to(jnp.max(lane_max, axis=1, keepdims=True), (2 * nq, LANE))
        ps = []
        for ci, tiles in enumerate(chunks):
            m = mb[ci * NA_QBW:(ci + 1) * NA_QBW, :]
            ps.append(jnp.concatenate(
                [jnp.exp2(tiles[g] - m).astype(BF16) if g in tiles else zero_tile for g in range(ng)], axis=1))
        o2 = _dot(jnp.concatenate(ps, axis=0), jnp.concatenate([vp, jnp.ones_like(vp)], axis=1))
        o = o2[:, 0:LANE] / o2[:, LANE:2 * LANE]
        out = jnp.where(lo, o[:nq], o[nq:]).astype(BF16)
        o_ref[0, :, :, sl] = out.reshape(NA_RB, NA_QBW, LANE)


def _na_attn_kernel(q_ref, k0, k1, k2, k3, v0, v1, v2, v3, bias_ref, rmask_ref, o_ref, *, nkb):
    rb = pl.program_id(2)
    kind = 2 * rb - _na_key_block0(rb, nkb)
    for var in range(3):
        pl.when(kind == var)(functools.partial(
            _na_attn_body, var, q_ref, (k0, k1, k2, k3), (v0, v1, v2, v3), bias_ref, rmask_ref, o_ref))


def _na_tables(rpb):
    n_dr = 2 * NA_KH - 1 + 16
    n_dc = 2 * NA_KW - 1 + 2 * NA_KW
    rpbp = jnp.pad(rpb.astype(F32) * LOG2E, ((0, 0), (8, 8), (NA_KW, NA_KW)))
    src = jnp.stack([rpbp[:, i:i + NA_NE, :] for i in range(NA_KSUB)], axis=2)
    src = src.reshape(NA_HEADS * NA_NE, NA_KSUB * n_dc)
    src = jnp.concatenate([src, jnp.full((NA_HEADS * NA_NE, 1), NEG, F32)], axis=1)
    j = np.arange(NA_NCB)[:, None, None]
    qc = np.arange(NA_QBW)[None, :, None]
    ln = np.arange(LANE)[None, None, :]
    kcol = np.asarray(NA_BAND_START)[j] + ln % NA_KBW
    c = NA_QBW * j + qc
    ws = np.clip(c - NA_KW // 2, 0, GRID_W - NA_KW)
    col_ok = (kcol >= ws) & (kcol < ws + NA_KW)
    pick = np.where(col_ok, (ln // NA_KBW) * n_dc + (kcol - c + NA_KW - 1) + NA_KW, NA_KSUB * n_dc)
    assert n_dr == NA_NE + NA_KSUB - 1 and pick.min() >= 0
    onehot = (jnp.arange(NA_KSUB * n_dc + 1, dtype=jnp.int32)[:, None]
              == jnp.asarray(pick.reshape(1, -1).astype(np.int32))).astype(F32)
    tab = jnp.dot(src, onehot, precision=lax.Precision.HIGHEST)
    tab = tab.reshape(NA_HEADS, NA_NE, NA_NCB, NA_QBW, LANE)

    var = np.arange(3)[:, None, None, None]
    qr = np.arange(NA_RB)[None, :, None, None]
    g = np.arange(NA_KROWS // NA_KSUB)[None, None, :, None]
    kr = NA_KSUB * g + np.arange(LANE)[None, None, None, :] // NA_KBW
    rs = np.where(var == 0, np.maximum(qr - NA_KH // 2, 0),
                  np.where(var == 1, qr, np.minimum(qr + NA_KH // 2, NA_KH)))
    row_ok = (kr >= rs) & (kr < rs + NA_KH)
    rmask = np.where(row_ok, 0.0, NEG).astype(np.float32)
    rmask = np.broadcast_to(rmask[:, :, :, None, :], (3, NA_RB, NA_KROWS // NA_KSUB, NA_QBW, LANE))
    return tab, jnp.asarray(np.ascontiguousarray(rmask))


def _na_attn(q, kb, vb, rpb):
    b, t, _ = q.shape
    rows = t // GRID_W
    assert rows % NA_RB == 0 and rows >= NA_KROWS
    nkb = rows // NA_KSUB
    q4 = q.reshape(b, rows, GRID_W, D_MODEL)
    tab, rmask = _na_tables(rpb)
    q_spec = pl.BlockSpec((1, NA_RB, NA_QBW, D_MODEL), lambda j, i, r: (i, r, j, 0))

    def band_spec(off):
        return pl.BlockSpec((1, NA_KSUB, 1, NA_KBW, D_MODEL),
                            lambda j, i, r: (i, _na_key_block0(r, nkb) + off, j, 0, 0))

    o = pl.pallas_call(
        functools.partial(_na_attn_kernel, nkb=nkb),
        out_shape=jax.ShapeDtypeStruct((b, rows, GRID_W, D_MODEL), BF16),
        grid=(NA_NCB, b, rows // NA_RB),
        in_specs=[q_spec] + [band_spec(off) for off in range(4)] * 2 + [
            pl.BlockSpec((NA_HEADS, NA_NE, None, NA_QBW, LANE), lambda j, i, r: (0, 0, j, 0, 0)),
            _resident(rmask.shape)],
        out_specs=q_spec,
        compiler_params=_cparams("parallel", "parallel", "parallel"),
        name="na_attn",
    )(q4, kb, kb, kb, kb, vb, vb, vb, vb, tab, rmask)
    return o.reshape(b, t, D_MODEL)


def _na_layer(x, g, w_qkv, rpb, w_o):
    b, t, _ = x.shape
    q, kb, vb = _na_qkv(x, g, w_qkv.astype(BF16))
    return _na_attn(q, kb, vb, rpb).reshape(b * t, D_MODEL), w_o.astype(BF16)


def _lru_in_kernel(x_ref, prev_ref, next_ref, g_ref, w_ref, cw_ref, cb_ref, gate_ref, xc_ref, *, tiles_per_seq):
    ti = pl.program_id(0) % tiles_per_seq
    tt = TOK_TILE
    n_ext = tt + 2 * HALO
    xe = jnp.concatenate([jnp.where(ti > 0, prev_ref[...], 0.0), x_ref[...],
                          jnp.where(ti < tiles_per_seq - 1, next_ref[...], 0.0)], axis=0)
    xn = _rms(xe, g_ref[...]).astype(BF16)
    for c0 in range(0, LRU_WIDTH, LRU_NTILE):
        cols = slice(c0, min(c0 + LRU_NTILE, LRU_WIDTH))
        gate_ref[:, cols] = jax.nn.gelu(_dot(xn[HALO:HALO + tt, :], w_ref[:, cols]), approximate=True)
        ext = _dot(xn, w_ref[:, LRU_WIDTH + cols.start:LRU_WIDTH + cols.stop])
        xc = cb_ref[:, cols] + cw_ref[2:3, cols] * ext[HALO:HALO + tt, :]
        for k in (0, 1, 3):
            xc = xc + cw_ref[k:k + 1, cols] * pltpu.roll(ext, (2 - k) % n_ext, 0)[HALO:HALO + tt, :]
        xc_ref[:, cols] = xc


def _lru_in(x, g, w, conv_w, conv_b, t):
    m = x.shape[0]
    hpt = TOK_TILE // HALO
    out = jax.ShapeDtypeStruct((m, LRU_WIDTH), F32)
    spec = pl.BlockSpec((TOK_TILE, LRU_WIDTH), lambda i: (i, 0))
    return pl.pallas_call(
        functools.partial(_lru_in_kernel, tiles_per_seq=t // TOK_TILE),
        out_shape=(out, out),
        grid=(m // TOK_TILE,),
        in_specs=[pl.BlockSpec((TOK_TILE, D_MODEL), lambda i: (i, 0)),
                  pl.BlockSpec((HALO, D_MODEL), lambda i: (jnp.maximum(i * hpt - 1, 0), 0)),
                  pl.BlockSpec((HALO, D_MODEL), lambda i: (jnp.minimum((i + 1) * hpt, m // HALO - 1), 0)),
                  _resident((1, D_MODEL)),
                  _resident((D_MODEL, 2 * LRU_WIDTH)),
                  _resident((CONV_W, LRU_WIDTH)),
                  _resident((1, LRU_WIDTH))],
        out_specs=(spec, spec),
        compiler_params=_cparams("parallel"),
        name="lru_in",
    )(x, x, x, g.reshape(1, D_MODEL), w, conv_w, conv_b.reshape(1, LRU_WIDTH))


def _lru_scan_tile(xc_ref, vec_ref, wa_ref, wx_ref,
                   a_ref, u_ref, h_ref, carry_ref, *, tile, reverse):
    step = pl.program_id(1)
    tt = tile
    n_groups = a_ref.shape[0]

    @pl.when(step == 0)
    def _():
        carry_ref[...] = jnp.zeros_like(carry_ref)
        a_ref[n_groups - 1] = jnp.zeros(a_ref.shape[1:], F32)
        u_ref[n_groups - 1] = jnp.zeros(u_ref.shape[1:], F32)

    xc = xc_ref[0]
    xcb = xc.astype(BF16)
    z = -vec_ref[2:3, :]
    c_pos = 0.5 * LRU_C * (jnp.maximum(z, 0.0) + jnp.log1p(jnp.exp(-jnp.abs(z))))
    c_exp = c_pos * (-LOG2E)
    for c, k0 in enumerate(LRU_BAND_START):
        w = min(LRU_NTILE, LRU_WIDTH - c * LRU_NTILE)
        cols = slice(c * LRU_NTILE, c * LRU_NTILE + w)
        xw = xcb[:, k0:k0 + LRU_KWIN]
        t_a = jnp.tanh(_dot(xw, wa_ref[c, :, 0:w]) + 0.5 * vec_ref[0:1, cols])
        t_x = jnp.tanh(_dot(xw, wx_ref[c, :, 0:w]) + 0.5 * vec_ref[1:2, cols])
        t1 = t_a + 1.0
        a = jnp.exp2(t1 * c_exp[:, cols])
        th = jnp.tanh(t1 * c_pos[:, cols])
        v = th * (a * a + 1.0)
        u = jnp.where(v > 0.0, v * lax.rsqrt(v), 0.0) * ((t_x + 1.0) * (0.5 * xc[:, cols]))
        for j in range(w // LANE):
            g, s = divmod(c * (LRU_NTILE // LANE) + j, 8)
            a_ref[g, pl.ds(s, tt, stride=8), :] = a[:, j * LANE:(j + 1) * LANE]
            u_ref[g, pl.ds(s, tt, stride=8), :] = u[:, j * LANE:(j + 1) * LANE]

    def body(k, hs):
        pair = (tt // 2 - 1 - k) if reverse else k
        rows = pl.ds(pl.multiple_of(pair * 16, 16), 16)
        out = []
        for g in range(n_groups):
            a2, u2 = a_ref[g, rows, :], u_ref[g, rows, :]
            first, second = (slice(8, 16), slice(0, 8)) if reverse else (slice(0, 8), slice(8, 16))
            a0, u0, a1, u1 = a2[first], u2[first], a2[second], u2[second]
            h0 = a0 * hs[g] + u0
            h1 = (a1 * a0) * hs[g] + (a1 * u0 + u1)
            h_ref[g, rows, :] = jnp.concatenate([h1, h0] if reverse else [h0, h1], axis=0)
            out.append(h1)
        return tuple(out)

    hs = lax.fori_loop(0, tt // 2, body, tuple(carry_ref[g] for g in range(n_groups)), unroll=4)
    for g in range(n_groups):
        carry_ref[g] = hs[g]
    return jnp.concatenate([h_ref[k // 8, pl.ds(k % 8, tt, stride=8), :] for k in range(LRU_WIDTH // LANE)], axis=1)


def _lru_bwd_kernel(xc_ref, vec_ref, wa_ref, wx_ref, hb_ref, a_ref, u_ref, h_ref, carry_ref, *, tile):
    hb_ref[0] = _lru_scan_tile(xc_ref, vec_ref, wa_ref, wx_ref, a_ref, u_ref, h_ref, carry_ref,
                               tile=tile, reverse=True)


def _lru_fwd_kernel(xc_ref, vec_ref, wa_ref, wx_ref, hb_ref, gate_ref, res_ref, wo_ref,
                    o_ref, a_ref, u_ref, h_ref, carry_ref, *, tile):
    h = _lru_scan_tile(xc_ref, vec_ref, wa_ref, wx_ref, a_ref, u_ref, h_ref, carry_ref, tile=tile, reverse=False)
    y = gate_ref[0] * (h + hb_ref[0])
    o_ref[0] = res_ref[0] + _dot(y.astype(BF16), wo_ref[...])


def _gate_bands(w):
    half = (0.5 * w).astype(BF16)
    bands = jnp.zeros((len(LRU_BAND_START), LRU_KWIN, LRU_NTILE), BF16)
    for c, k0 in enumerate(LRU_BAND_START):
        lo, hi = c * LRU_NTILE, min((c + 1) * LRU_NTILE, LRU_WIDTH)
        for n in range(lo // LRU_BLOCK, (hi - 1) // LRU_BLOCK + 1):
            r0 = n * LRU_BLOCK - k0
            assert 0 <= r0 and r0 + LRU_BLOCK <= LRU_KWIN
            c0, c1 = max(n * LRU_BLOCK, lo), min((n + 1) * LRU_BLOCK, hi)
            bands = bands.at[c, r0:r0 + LRU_BLOCK, c0 - lo:c1 - lo].set(
                half[n, :, c0 - n * LRU_BLOCK:c1 - n * LRU_BLOCK])
    return bands


def _lru_layer(x, g, w_in, conv_w, conv_b, w_a, b_a, w_x, b_x, lam, w_out):
    b, t, _ = x.shape
    c = LRU_WIDTH
    tile = TOK_TILE
    n_tiles = t // tile
    gate, xc = _lru_in(x.reshape(b * t, D_MODEL), g, w_in.astype(BF16), conv_w, conv_b, t)
    gate = gate.reshape(b, t, c)
    xc = xc.reshape(b, t, c)
    band_shape = (len(LRU_BAND_START), LRU_KWIN, LRU_NTILE)

    def scan_specs(reverse):
        return [pl.BlockSpec((1, tile, c), lambda i, s: (i, (n_tiles - 1 - s) if reverse else s, 0)),
                _resident((3, c)), _resident(band_shape), _resident(band_shape)]

    n_groups = -(-(c // LANE) // 8)
    slab_rows = pltpu.VMEM((n_groups, tile * 8, LANE), F32)
    scratch = [slab_rows, slab_rows, slab_rows, pltpu.VMEM((n_groups, 8, LANE), F32)]

    def direction_params(d):
        vec = jnp.stack([b_a[d], b_x[d], lam[d]]).astype(F32)
        return vec, _gate_bands(w_a[d]), _gate_bands(w_x[d])

    hb = pl.pallas_call(
        functools.partial(_lru_bwd_kernel, tile=tile),
        out_shape=jax.ShapeDtypeStruct((b, t, c), F32),
        grid=(b, n_tiles),
        in_specs=scan_specs(True),
        out_specs=pl.BlockSpec((1, tile, c), lambda i, s: (i, n_tiles - 1 - s, 0)),
        scratch_shapes=scratch,
        compiler_params=_cparams("arbitrary", "arbitrary"),
        name="lru_bwd",
    )(xc, *direction_params(1))

    tok = lambda n: pl.BlockSpec((1, tile, n), lambda i, s: (i, s, 0))
    return pl.pallas_call(
        functools.partial(_lru_fwd_kernel, tile=tile),
        out_shape=jax.ShapeDtypeStruct((b, t, D_MODEL), F32),
        grid=(b, n_tiles),
        in_specs=scan_specs(False) + [tok(c), tok(c), tok(D_MODEL), _resident((c, D_MODEL))],
        out_specs=tok(D_MODEL),
        scratch_shapes=scratch,
        compiler_params=_cparams("arbitrary", "arbitrary"),
        name="lru_fwd",
    )(xc, *direction_params(0), hb, gate, x, w_out.astype(BF16)).reshape(b * t, D_MODEL)


def _rope_lanes(v, cos, sin):
    return v * cos + pltpu.roll(v, LANE // 2, 1) * sin


def _mla_proj_kernel(x_ref, g_ref, wdq_ref, gq_ref, wuq_ref, wdkv_ref, gkv_ref, wukv_ref, cos_ref, sin_ref,
                     qn_ref, qr_ref, kn_ref, kr_ref, v_ref):
    scale = (MLA_NOPE + MLA_ROPE) ** -0.5 * LOG2E
    nq = MLA_HEADS * MLA_NOPE
    xn = _rms(x_ref[0], g_ref[...]).astype(BF16)
    cos = cos_ref[...]
    sin = sin_ref[...]
    cq = _rms(_dot(xn, wdq_ref[...]), gq_ref[...]).astype(BF16)
    q = _dot(cq, wuq_ref[...])
    qn_ref[0] = (q[:, 0:nq] * scale).astype(BF16)
    for p in range(MLA_HEADS // 2):
        sl = slice(nq + p * LANE, nq + (p + 1) * LANE)
        qr_ref[0, :, p * LANE:(p + 1) * LANE] = (_rope_lanes(q[:, sl], cos, sin) * scale).astype(BF16)
    kva = _dot(xn, wdkv_ref[...])
    kr_ref[0] = _rope_lanes(kva[:, MLA_KV_RANK:MLA_KV_RANK + LANE], cos, sin).astype(BF16)
    ckv = _rms(kva[:, 0:MLA_KV_RANK], gkv_ref[...]).astype(BF16)
    kv = _dot(ckv, wukv_ref[...])
    kn_ref[0] = kv[:, 0:nq].astype(BF16)
    v_ref[0] = kv[:, nq:2 * nq].astype(BF16)


def _mla_attn_kernel(qn_ref, qr_ref, kn_ref, kr_ref, v_ref, o_ref,
                     ql_ref, s_ref, p_ref, mxb_ref, m_ref, l_ref, acc_ref, *, seq):
    tq, tk, ch = MLA_TQ, MLA_TK, MLA_CH
    rows = 2 * tq
    n = seq // tk
    assert n >= 2 and n % 2 == 0
    lane = lax.broadcasted_iota(jnp.int32, (1, 2 * LANE), 1)
    in_a = (lane < MLA_NOPE) | ((lane >= LANE) & (lane < LANE + MLA_ROPE))
    in_b = ((lane >= MLA_NOPE) & (lane < LANE)) | ((lane >= LANE + MLA_ROPE) & (lane < LANE + 2 * MLA_ROPE))
    lo = lax.broadcasted_iota(jnp.int32, (1, LANE), 1) < MLA_V

    def lanes(pp):
        return slice(pp * LANE, (pp + 1) * LANE)

    for pp in range(MLA_PAIRS):
        qp = jnp.concatenate([qn_ref[0, :, lanes(pp)], qr_ref[0, :, lanes(pp)]], axis=1)
        zero = jnp.zeros_like(qp)
        ql_ref[pp, 0:tq, :] = jnp.where(in_a, qp, zero)
        ql_ref[pp, tq:rows, :] = jnp.where(in_b, qp, zero)

    def key_rows(kt):
        return pl.ds(kt * tk if isinstance(kt, int) else pl.multiple_of(kt * tk, tk), tk)

    def qk(pp, kt, slot):
        kk = jnp.concatenate([kn_ref[0, key_rows(kt), lanes(pp)], kr_ref[0, key_rows(kt), :]], axis=1)
        s = _dot_nt(ql_ref[pp], kk)
        s_ref[slot] = s
        mxb_ref[slot] = jnp.broadcast_to(jnp.max(s, axis=1, keepdims=True), (rows, LANE))

    def pv(pp, kt, slot):
        return _dot(p_ref[slot], v_ref[0, key_rows(kt), lanes(pp)])

    def finish(pp, o_last):
        o = (acc_ref[pp] + o_last) / jnp.sum(l_ref[pp], axis=1, keepdims=True)
        o_ref[0, :, lanes(pp)] = jnp.where(lo, o[:tq], o[tq:]).astype(BF16)

    def stage(pp, kt, slot, nxt):
        first = isinstance(kt, int) and kt == 0
        if nxt is not None:
            qk(nxt[0], nxt[1], 1 - slot)
        o_prev = None
        if not first:
            o_prev = pv(pp, kt - 1, 1 - slot)
        elif pp > 0:
            finish(pp - 1, pv(pp - 1, n - 1, 1 - slot))
        if first:
            m_new = mxb_ref[slot]
        else:
            m_old = m_ref[pp]
            m_new = jnp.maximum(m_old, mxb_ref[slot])
            alpha = jnp.exp2(m_old - m_new)
        m_ref[pp] = m_new
        for c in range(rows // ch):
            r = slice(c * ch, (c + 1) * ch)
            ps = [jnp.exp2(s_ref[slot, r, g * LANE:(g + 1) * LANE] - m_new[r, :]) for g in range(tk // LANE)]
            p_ref[slot, r, :] = jnp.concatenate(ps, axis=1).astype(BF16)
            psum = functools.reduce(lambda a, b: a + b, ps)
            if first:
                l_ref[pp, r, :] = psum
                acc_ref[pp, r, :] = jnp.zeros((ch, LANE), F32)
            else:
                l_ref[pp, r, :] = alpha[r, :] * l_ref[pp, r, :] + psum
                acc_ref[pp, r, :] = alpha[r, :] * (acc_ref[pp, r, :] + o_prev[r, :])

    tiles = [(pp, kt) for pp in range(MLA_PAIRS) for kt in range(n)]
    qk(0, 0, 0)
    for i, (pp, kt) in enumerate(tiles):
        stage(pp, kt, kt % 2, tiles[i + 1] if i + 1 < len(tiles) else None)
    finish(MLA_PAIRS - 1, pv(MLA_PAIRS - 1, n - 1, (n - 1) % 2))


def _mla_weights(w_uq, w_dkv, w_ukv):
    h, nope, rope = MLA_HEADS, MLA_NOPE, MLA_ROPE
    half = rope // 2
    wq = w_uq.reshape(MLA_Q_RANK, h, nope + rope)
    wq_nope = wq[:, :, :nope].reshape(MLA_Q_RANK, h * nope)
    wq_rope = wq[:, :, nope:].reshape(MLA_Q_RANK, h // 2, 2, rope)
    wq_swap = jnp.concatenate([wq_rope[..., half:], wq_rope[..., :half]], axis=-1)
    wq_rope = jnp.concatenate([wq_rope.reshape(MLA_Q_RANK, h // 2, 2 * rope),
                               wq_swap.reshape(MLA_Q_RANK, h // 2, 2 * rope)], axis=-1)
    wuq = jnp.concatenate([wq_nope, wq_rope.reshape(MLA_Q_RANK, h // 2 * LANE)], axis=1)
    w_kr = w_dkv[:, MLA_KV_RANK:]
    w_krs = jnp.concatenate([w_kr[:, half:], w_kr[:, :half]], axis=1)
    wdkv = jnp.concatenate([w_dkv[:, :MLA_KV_RANK], w_kr, w_kr, w_krs, w_krs], axis=1)
    wkv = w_ukv.reshape(MLA_KV_RANK, h, nope + MLA_V)
    wukv = jnp.concatenate([wkv[:, :, :nope].reshape(MLA_KV_RANK, h * nope),
                            wkv[:, :, nope:].reshape(MLA_KV_RANK, h * MLA_V)], axis=1)
    return wuq.astype(BF16), wdkv.astype(BF16), wukv.astype(BF16)


def _rope_tables(t):
    pos = jnp.arange(t, dtype=F32)
    inv = ROPE_THETA ** (-jnp.arange(0, MLA_ROPE, 2, dtype=F32) / MLA_ROPE)
    ang = pos[:, None] * inv[None, :]
    cos, sin = jnp.cos(ang), jnp.sin(ang)
    pad = jnp.zeros((t, LANE // 2), F32)
    return (jnp.concatenate([cos, cos, cos, cos, pad], axis=1),
            jnp.concatenate([-sin, sin, -sin, sin, pad], axis=1))


def _mla_layer(x, g, w_dq, g_q, w_uq, w_dkv, g_kv, w_ukv, w_o):
    b, t, _ = x.shape
    tm = TOK_TILE
    wuq, wdkv, wukv = _mla_weights(w_uq, w_dkv, w_ukv)
    cos, sin = _rope_tables(t)
    wide = MLA_HEADS * MLA_NOPE
    tok = lambda n: pl.BlockSpec((1, tm, n), lambda i, s: (i, s, 0))
    big = jax.ShapeDtypeStruct((b, t, wide), BF16)
    qn, qr, kn, kr, v = pl.pallas_call(
        _mla_proj_kernel,
        out_shape=(big, big, big, jax.ShapeDtypeStruct((b, t, LANE), BF16), big),
        grid=(b, t // tm),
        in_specs=[tok(D_MODEL), _resident((1, D_MODEL)),
                  _resident((D_MODEL, MLA_Q_RANK)), _resident((1, MLA_Q_RANK)), _resident(wuq.shape),
                  _resident(wdkv.shape), _resident((1, MLA_KV_RANK)), _resident(wukv.shape),
                  pl.BlockSpec((tm, LANE), lambda i, s: (s, 0)), pl.BlockSpec((tm, LANE), lambda i, s: (s, 0))],
        out_specs=(tok(wide), tok(wide), tok(wide), tok(LANE), tok(wide)),
        compiler_params=_cparams("parallel", "parallel"),
        name="mla_proj",
    )(x, g.reshape(1, D_MODEL), w_dq.astype(BF16), g_q.reshape(1, MLA_Q_RANK), wuq,
      wdkv, g_kv.reshape(1, MLA_KV_RANK), wukv, cos, sin)

    width = MLA_PAIRS * LANE
    tk = MLA_TK
    rows = 2 * MLA_TQ
    q_spec = pl.BlockSpec((1, MLA_TQ, width), lambda i, p, s: (i, s, p))
    kv_spec = pl.BlockSpec((1, t, width), lambda i, p, s: (i, 0, p))
    o = pl.pallas_call(
        functools.partial(_mla_attn_kernel, seq=t),
        out_shape=jax.ShapeDtypeStruct((b, t, MLA_HEADS * MLA_V), BF16),
        grid=(b, MLA_HEADS // 2 // MLA_PAIRS, t // MLA_TQ),
        in_specs=[q_spec, q_spec, kv_spec,
                  pl.BlockSpec((1, t, LANE), lambda i, p, s: (i, 0, 0)), kv_spec],
        out_specs=q_spec,
        scratch_shapes=[pltpu.VMEM((MLA_PAIRS, rows, 2 * LANE), BF16),
                        pltpu.VMEM((2, rows, tk), F32),
                        pltpu.VMEM((2, rows, tk), BF16),
                        pltpu.VMEM((2, rows, LANE), F32),
                        pltpu.VMEM((MLA_PAIRS, rows, LANE), F32),
                        pltpu.VMEM((MLA_PAIRS, rows, LANE), F32),
                        pltpu.VMEM((MLA_PAIRS, rows, LANE), F32)],
        compiler_params=_cparams("parallel", "parallel", "parallel"),
        name="mla_attn",
    )(qn, qr, kn, kr, v)
    return o.reshape(b * t, MLA_HEADS * MLA_V), w_o.astype(BF16)


def kernel(x_prompt, x_sample, norm_mix, norm_ffn, norm_final, na_w_qkv, na_rpb, na_w_o, lru_w_in, lru_conv_w, lru_conv_b, lru_w_a, lru_b_a, lru_w_x, lru_b_x, lru_lam, lru_w_out, mla_w_dq, mla_g_q, mla_w_uq, mla_w_dkv, mla_g_kv, mla_w_ukv, mla_w_o, ffn_w_gate, ffn_w_up, ffn_w_down):
    wg = ffn_w_gate.astype(BF16)
    wu = ffn_w_up.astype(BF16)
    wd = ffn_w_down.astype(BF16)

    def run(x):
        b, t, _ = x.shape
        x = x.reshape(b * t, D_MODEL)
        for i in range(DEPTH):
            j, m = divmod(i, N_MIXERS)
            x3 = x.reshape(b, t, D_MODEL)
            attn = w_o = None
            if m == 0:
                attn, w_o = _na_layer(x3, norm_mix[i], na_w_qkv[j], na_rpb[j], na_w_o[j])
            elif m == 1:
                x = _lru_layer(x3, norm_mix[i], lru_w_in[j], lru_conv_w[j], lru_conv_b[j], lru_w_a[j], lru_b_a[j],
                               lru_w_x[j], lru_b_x[j], lru_lam[j], lru_w_out[j])
            else:
                attn, w_o = _mla_layer(x3, norm_mix[i], mla_w_dq[j], mla_g_q[j], mla_w_uq[j], mla_w_dkv[j],
                                       mla_g_kv[j], mla_w_ukv[j], mla_w_o[j])
            x = _ffn(x, norm_ffn[i], i, wg, wu, wd, norm_final if i == DEPTH - 1 else None, attn, w_o)
        return x.reshape(b, t, D_MODEL)

    return run(x_prompt), run(x_sample)
```

```python
import functools

import numpy as np
import jax
import jax.numpy as jnp
from jax import lax
from jax.experimental import pallas as pl
from jax.experimental.pallas import tpu as pltpu

F32 = jnp.float32
BF16 = jnp.bfloat16

D_MODEL = 1024
DEPTH = 4
N_MIXERS = 3
RMS_EPS = 1e-6
D_FF = 2816

GRID_W = 64
NA_HEADS = 16
NA_HEAD_DIM = 64
NA_KH = 8
NA_KW = 16
NA_QBW = 16
NA_KBW = 32
NA_NCB = GRID_W // NA_QBW
NA_BAND_START = tuple(int(v) for v in np.clip(np.arange(NA_NCB) * NA_QBW - NA_KW // 2, 0, GRID_W - NA_KBW))
NA_RB = 8
NA_KROWS = 16
NA_KSUB = 4
NA_NE = 28
NEG = -1e30
LOG2E = float(np.log2(np.e))

LRU_WIDTH = 1408
LRU_BLOCKS = 16
LRU_BLOCK = 88
CONV_W = 4
LRU_C = 8.0
HALO = 8
LRU_NTILE = 256
LRU_KWIN = 512
LRU_BAND_START = tuple(min(max(c * LRU_NTILE - 128, 0), LRU_WIDTH - LRU_KWIN)
                       for c in range(-(-LRU_WIDTH // LRU_NTILE)))

MLA_HEADS = 16
MLA_Q_RANK = 384
MLA_KV_RANK = 256
MLA_NOPE = 64
MLA_ROPE = 32
MLA_V = 64
ROPE_THETA = 10000.0

LANE = 128
TOK_TILE = 512
FFN_TILE = 1024
FF_CHUNK = 256
MLA_TQ = 512
MLA_TK = 512
MLA_CH = 32
MLA_PAIRS = 2
VMEM_LIMIT = 56 * 1024 * 1024


def _cparams(*sem):
    return pltpu.CompilerParams(dimension_semantics=sem, vmem_limit_bytes=VMEM_LIMIT)


def _resident(shape):
    nd = len(shape)
    return pl.BlockSpec(shape, lambda *_: (0,) * nd, pipeline_mode=pl.Buffered(1))


def _rms(x, g):
    return x * lax.rsqrt(jnp.mean(x * x, axis=-1, keepdims=True) + RMS_EPS) * g


def _dot(a, b):
    return jnp.dot(a, b, preferred_element_type=F32)


def _dot_nt(a, b):
    return lax.dot_general(a, b, (((1,), (1,)), ((), ())), preferred_element_type=F32)


def _ffn_kernel(*refs, mixer, final):
    x_ref, g_ref, wg_ref, wu_ref, wd_ref = refs[:5]
    o_ref = refs[-1]
    x = x_ref[...]
    if mixer:
        x = x + _dot(refs[5][...], refs[6][...])
    xn = _rms(x, g_ref[...]).astype(BF16)
    acc = x
    for c in range(D_FF // FF_CHUNK):
        sl = slice(c * FF_CHUNK, (c + 1) * FF_CHUNK)
        g = _dot(xn, wg_ref[:, sl])
        u = _dot(xn, wu_ref[:, sl])
        h = (g * jax.nn.sigmoid(g) * u).astype(BF16)
        acc = acc + _dot(h, wd_ref[sl, :])
    if final:
        acc = _rms(acc, refs[-2][...])
    o_ref[...] = acc


def _ffn(x, g, layer, wg, wu, wd, g_final=None, attn=None, w_o=None):
    m = x.shape[0]
    final = g_final is not None
    mixer = attn is not None
    ins = [x, g.reshape(1, D_MODEL), wg, wu, wd]

    def layer_spec(rows, cols):
        return pl.BlockSpec((None, rows, cols), lambda i: (layer, 0, 0), pipeline_mode=pl.Buffered(1))

    specs = [pl.BlockSpec((FFN_TILE, D_MODEL), lambda i: (i, 0)),
             _resident((1, D_MODEL)),
             layer_spec(D_MODEL, D_FF), layer_spec(D_MODEL, D_FF), layer_spec(D_FF, D_MODEL)]
    if mixer:
        ins += [attn, w_o]
        specs += [pl.BlockSpec((FFN_TILE, attn.shape[1]), lambda i: (i, 0)), _resident(w_o.shape)]
    if final:
        ins.append(g_final.reshape(1, D_MODEL))
        specs.append(_resident((1, D_MODEL)))
    return pl.pallas_call(
        functools.partial(_ffn_kernel, mixer=mixer, final=final),
        out_shape=jax.ShapeDtypeStruct((m, D_MODEL), F32),
        grid=(m // FFN_TILE,),
        in_specs=specs,
        out_specs=pl.BlockSpec((FFN_TILE, D_MODEL), lambda i: (i, 0)),
        compiler_params=_cparams("parallel"),
        name="ffn",
    )(*ins)


def _na_qkv_kernel(x_ref, g_ref, w_ref, q_ref, kb_ref, vb_ref):
    xn = _rms(x_ref[0], g_ref[...]).astype(BF16)
    q = _dot(xn, w_ref[:, 0:D_MODEL])
    q_ref[0] = (q * (NA_HEAD_DIM ** -0.5 * LOG2E)).astype(BF16)
    for part, out_ref in ((1, kb_ref), (2, vb_ref)):
        r = _dot(xn, w_ref[:, part * D_MODEL:(part + 1) * D_MODEL])
        r3 = r.reshape(NA_RB, GRID_W, D_MODEL)
        for j, bs in enumerate(NA_BAND_START):
            out_ref[0, :, j] = r3[:, bs:bs + NA_KBW, :].astype(BF16)


def _na_qkv(x, g, w):
    b, t, _ = x.shape
    rows = t // GRID_W
    tile = NA_RB * GRID_W
    band = jax.ShapeDtypeStruct((b, rows, NA_NCB, NA_KBW, D_MODEL), BF16)
    band_spec = pl.BlockSpec((1, NA_RB, NA_NCB, NA_KBW, D_MODEL), lambda i, r: (i, r, 0, 0, 0))
    return pl.pallas_call(
        _na_qkv_kernel,
        out_shape=(jax.ShapeDtypeStruct((b, t, D_MODEL), BF16), band, band),
        grid=(b, rows // NA_RB),
        in_specs=[pl.BlockSpec((1, tile, D_MODEL), lambda i, r: (i, r, 0)),
                  _resident((1, D_MODEL)),
                  _resident((D_MODEL, 3 * D_MODEL))],
        out_specs=(pl.BlockSpec((1, tile, D_MODEL), lambda i, r: (i, r, 0)), band_spec, band_spec),
        compiler_params=_cparams("parallel", "parallel"),
        name="na_qkv",
    )(x, g.reshape(1, D_MODEL), w)


def _na_key_block0(rb, nkb):
    return jnp.clip(2 * rb - 1, 0, nkb - NA_KROWS // NA_KSUB)


def _na_first_key_row(var, qr):
    half = NA_KH // 2
    return (max(qr - half, 0), qr, min(qr + half, NA_KH))[var]


def _na_attn_body(var, q_ref, k_refs, v_refs, bias_ref, rmask_ref, o_ref):
    lo = lax.broadcasted_iota(jnp.int32, (1, LANE), 1) < NA_HEAD_DIM
    nq = NA_RB * NA_QBW
    ng = NA_KROWS // NA_KSUB
    zero_tile = jnp.zeros((NA_QBW, LANE), BF16)
    for hp in range(NA_HEADS // 2):
        sl = slice(hp * LANE, (hp + 1) * LANE)
        qp = q_ref[0, :, :, sl].reshape(nq, LANE)
        zero = jnp.zeros_like(qp)
        ql = jnp.concatenate([jnp.where(lo, qp, zero), jnp.where(lo, zero, qp)], axis=0)
        kp = jnp.concatenate([r[0, :, 0, :, sl].reshape(NA_KSUB * NA_KBW, LANE) for r in k_refs], axis=0)
        vp = jnp.concatenate([r[0, :, 0, :, sl].reshape(NA_KSUB * NA_KBW, LANE) for r in v_refs], axis=0)
        s = _dot_nt(ql, kp)
        chunks = []
        for a in range(2):
            for qr in range(NA_RB):
                r0 = a * nq + qr * NA_QBW
                rs = _na_first_key_row(var, qr)
                tiles = {}
                for g in range(ng):
                    if NA_KSUB * (g + 1) > rs and NA_KSUB * g < rs + NA_KH:
                        t = (s[r0:r0 + NA_QBW, g * LANE:(g + 1) * LANE]
                             + bias_ref[2 * hp + a, 4 * g - qr + 15 - 4 * var])
                        if not (NA_KSUB * g >= rs and NA_KSUB * (g + 1) <= rs + NA_KH):
                            t = t + rmask_ref[var, qr, g]
                        tiles[g] = t
                chunks.append(tiles)
        lane_max = jnp.concatenate([functools.reduce(jnp.maximum, tiles.values()) for tiles in chunks], axis=0)
        mb = jnp.broadcast_to(jnp.max(lane_max, axis=1, keepdims=True), (2 * nq, LANE))
        ps = []
        for ci, tiles in enumerate(chunks):
            m = mb[ci * NA_QBW:(ci + 1) * NA_QBW, :]
            ps.append(jnp.concatenate(
                [jnp.exp2(tiles[g] - m).astype(BF16) if g in tiles else zero_tile for g in range(ng)], axis=1))
        o2 = _dot(jnp.concatenate(ps, axis=0), jnp.concatenate([vp, jnp.ones_like(vp)], axis=1))
        o = o2[:, 0:LANE] / o2[:, LANE:2 * LANE]
        out = jnp.where(lo, o[:nq], o[nq:]).astype(BF16)
        o_ref[0, :, :, sl] = out.reshape(NA_RB, NA_QBW, LANE)


def _na_attn_kernel(q_ref, k0, k1, k2, k3, v0, v1, v2, v3, bias_ref, rmask_ref, o_ref, *, nkb):
    rb = pl.program_id(2)
    kind = 2 * rb - _na_key_block0(rb, nkb)
    for var in range(3):
        pl.when(kind == var)(functools.partial(
            _na_attn_body, var, q_ref, (k0, k1, k2, k3), (v0, v1, v2, v3), bias_ref, rmask_ref, o_ref))


def _na_tables(rpb):
    n_dr = 2 * NA_KH - 1 + 16
    n_dc = 2 * NA_KW - 1 + 2 * NA_KW
    rpbp = jnp.pad(rpb.astype(F32) * LOG2E, ((0, 0), (8, 8), (NA_KW, NA_KW)))
    src = jnp.stack([rpbp[:, i:i + NA_NE, :] for i in range(NA_KSUB)], axis=2)
    src = src.reshape(NA_HEADS * NA_NE, NA_KSUB * n_dc)
    src = jnp.concatenate([src, jnp.full((NA_HEADS * NA_NE, 1), NEG, F32)], axis=1)
    j = np.arange(NA_NCB)[:, None, None]
    qc = np.arange(NA_QBW)[None, :, None]
    ln = np.arange(LANE)[None, None, :]
    kcol = np.asarray(NA_BAND_START)[j] + ln % NA_KBW
    c = NA_QBW * j + qc
    ws = np.clip(c - NA_KW // 2, 0, GRID_W - NA_KW)
    col_ok = (kcol >= ws) & (kcol < ws + NA_KW)
    pick = np.where(col_ok, (ln // NA_KBW) * n_dc + (kcol - c + NA_KW - 1) + NA_KW, NA_KSUB * n_dc)
    assert n_dr == NA_NE + NA_KSUB - 1 and pick.min() >= 0
    onehot = (jnp.arange(NA_KSUB * n_dc + 1, dtype=jnp.int32)[:, None]
              == jnp.asarray(pick.reshape(1, -1).astype(np.int32))).astype(F32)
    tab = jnp.dot(src, onehot, precision=lax.Precision.HIGHEST)
    tab = tab.reshape(NA_HEADS, NA_NE, NA_NCB, NA_QBW, LANE)

    var = np.arange(3)[:, None, None, None]
    qr = np.arange(NA_RB)[None, :, None, None]
    g = np.arange(NA_KROWS // NA_KSUB)[None, None, :, None]
    kr = NA_KSUB * g + np.arange(LANE)[None, None, None, :] // NA_KBW
    rs = np.where(var == 0, np.maximum(qr - NA_KH // 2, 0),
                  np.where(var == 1, qr, np.minimum(qr + NA_KH // 2, NA_KH)))
    row_ok = (kr >= rs) & (kr < rs + NA_KH)
    rmask = np.where(row_ok, 0.0, NEG).astype(np.float32)
    rmask = np.broadcast_to(rmask[:, :, :, None, :], (3, NA_RB, NA_KROWS // NA_KSUB, NA_QBW, LANE))
    return tab, jnp.asarray(np.ascontiguousarray(rmask))


def _na_attn(q, kb, vb, rpb):
    b, t, _ = q.shape
    rows = t // GRID_W
    assert rows % NA_RB == 0 and rows >= NA_KROWS
    nkb = rows // NA_KSUB
    q4 = q.reshape(b, rows, GRID_W, D_MODEL)
    tab, rmask = _na_tables(rpb)
    q_spec = pl.BlockSpec((1, NA_RB, NA_QBW, D_MODEL), lambda j, i, r: (i, r, j, 0))

    def band_spec(off):
        return pl.BlockSpec((1, NA_KSUB, 1, NA_KBW, D_MODEL),
                            lambda j, i, r: (i, _na_key_block0(r, nkb) + off, j, 0, 0))

    o = pl.pallas_call(
        functools.partial(_na_attn_kernel, nkb=nkb),
        out_shape=jax.ShapeDtypeStruct((b, rows, GRID_W, D_MODEL), BF16),
        grid=(NA_NCB, b, rows // NA_RB),
        in_specs=[q_spec] + [band_spec(off) for off in range(4)] * 2 + [
            pl.BlockSpec((NA_HEADS, NA_NE, None, NA_QBW, LANE), lambda j, i, r: (0, 0, j, 0, 0)),
            _resident(rmask.shape)],
        out_specs=q_spec,
        compiler_params=_cparams("parallel", "parallel", "parallel"),
        name="na_attn",
    )(q4, kb, kb, kb, kb, vb, vb, vb, vb, tab, rmask)
    return o.reshape(b, t, D_MODEL)


def _na_layer(x, g, w_qkv, rpb, w_o):
    b, t, _ = x.shape
    q, kb, vb = _na_qkv(x, g, w_qkv.astype(BF16))
    return _na_attn(q, kb, vb, rpb).reshape(b * t, D_MODEL), w_o.astype(BF16)


def _lru_in_kernel(x_ref, prev_ref, next_ref, g_ref, w_ref, cw_ref, cb_ref, gate_ref, xc_ref, *, tiles_per_seq):
    ti = pl.program_id(0) % tiles_per_seq
    tt = TOK_TILE
    n_ext = tt + 2 * HALO
    xe = jnp.concatenate([jnp.where(ti > 0, prev_ref[...], 0.0), x_ref[...],
                          jnp.where(ti < tiles_per_seq - 1, next_ref[...], 0.0)], axis=0)
    xn = _rms(xe, g_ref[...]).astype(BF16)
    for c0 in range(0, LRU_WIDTH, LRU_NTILE):
        cols = slice(c0, min(c0 + LRU_NTILE, LRU_WIDTH))
        gate_ref[:, cols] = jax.nn.gelu(_dot(xn[HALO:HALO + tt, :], w_ref[:, cols]), approximate=True)
        ext = _dot(xn, w_ref[:, LRU_WIDTH + cols.start:LRU_WIDTH + cols.stop])
        xc = cb_ref[:, cols] + cw_ref[2:3, cols] * ext[HALO:HALO + tt, :]
        for k in (0, 1, 3):
            xc = xc + cw_ref[k:k + 1, cols] * pltpu.roll(ext, (2 - k) % n_ext, 0)[HALO:HALO + tt, :]
        xc_ref[:, cols] = xc


def _lru_in(x, g, w, conv_w, conv_b, t):
    m = x.shape[0]
    hpt = TOK_TILE // HALO
    out = jax.ShapeDtypeStruct((m, LRU_WIDTH), F32)
    spec = pl.BlockSpec((TOK_TILE, LRU_WIDTH), lambda i: (i, 0))
    return pl.pallas_call(
        functools.partial(_lru_in_kernel, tiles_per_seq=t // TOK_TILE),
        out_shape=(out, out),
        grid=(m // TOK_TILE,),
        in_specs=[pl.BlockSpec((TOK_TILE, D_MODEL), lambda i: (i, 0)),
                  pl.BlockSpec((HALO, D_MODEL), lambda i: (jnp.maximum(i * hpt - 1, 0), 0)),
                  pl.BlockSpec((HALO, D_MODEL), lambda i: (jnp.minimum((i + 1) * hpt, m // HALO - 1), 0)),
                  _resident((1, D_MODEL)),
                  _resident((D_MODEL, 2 * LRU_WIDTH)),
                  _resident((CONV_W, LRU_WIDTH)),
                  _resident((1, LRU_WIDTH))],
        out_specs=(spec, spec),
        compiler_params=_cparams("parallel"),
        name="lru_in",
    )(x, x, x, g.reshape(1, D_MODEL), w, conv_w, conv_b.reshape(1, LRU_WIDTH))


def _lru_scan_tile(xc_ref, vec_ref, wa_ref, wx_ref,
                   a_ref, u_ref, h_ref, carry_ref, *, tile, reverse):
    step = pl.program_id(1)
    tt = tile
    n_groups = a_ref.shape[0]

    @pl.when(step == 0)
    def _():
        carry_ref[...] = jnp.zeros_like(carry_ref)
        a_ref[n_groups - 1] = jnp.zeros(a_ref.shape[1:], F32)
        u_ref[n_groups - 1] = jnp.zeros(u_ref.shape[1:], F32)

    xc = xc_ref[0]
    xcb = xc.astype(BF16)
    z = -vec_ref[2:3, :]
    c_pos = 0.5 * LRU_C * (jnp.maximum(z, 0.0) + jnp.log1p(jnp.exp(-jnp.abs(z))))
    c_exp = c_pos * (-LOG2E)
    for c, k0 in enumerate(LRU_BAND_START):
        w = min(LRU_NTILE, LRU_WIDTH - c * LRU_NTILE)
        cols = slice(c * LRU_NTILE, c * LRU_NTILE + w)
        xw = xcb[:, k0:k0 + LRU_KWIN]
        t_a = jnp.tanh(_dot(xw, wa_ref[c, :, 0:w]) + 0.5 * vec_ref[0:1, cols])
        t_x = jnp.tanh(_dot(xw, wx_ref[c, :, 0:w]) + 0.5 * vec_ref[1:2, cols])
        t1 = t_a + 1.0
        a = jnp.exp2(t1 * c_exp[:, cols])
        th = jnp.tanh(t1 * c_pos[:, cols])
        v = th * (a * a + 1.0)
        u = jnp.where(v > 0.0, v * lax.rsqrt(v), 0.0) * ((t_x + 1.0) * (0.5 * xc[:, cols]))
        for j in range(w // LANE):
            g, s = divmod(c * (LRU_NTILE // LANE) + j, 8)
            a_ref[g, pl.ds(s, tt, stride=8), :] = a[:, j * LANE:(j + 1) * LANE]
            u_ref[g, pl.ds(s, tt, stride=8), :] = u[:, j * LANE:(j + 1) * LANE]

    def body(k, hs):
        pair = (tt // 2 - 1 - k) if reverse else k
        rows = pl.ds(pl.multiple_of(pair * 16, 16), 16)
        out = []
        for g in range(n_groups):
            a2, u2 = a_ref[g, rows, :], u_ref[g, rows, :]
            first, second = (slice(8, 16), slice(0, 8)) if reverse else (slice(0, 8), slice(8, 16))
            a0, u0, a1, u1 = a2[first], u2[first], a2[second], u2[second]
            h0 = a0 * hs[g] + u0
            h1 = (a1 * a0) * hs[g] + (a1 * u0 + u1)
            h_ref[g, rows, :] = jnp.concatenate([h1, h0] if reverse else [h0, h1], axis=0)
            out.append(h1)
        return tuple(out)

    hs = lax.fori_loop(0, tt // 2, body, tuple(carry_ref[g] for g in range(n_groups)), unroll=4)
    for g in range(n_groups):
        carry_ref[g] = hs[g]
    return jnp.concatenate([h_ref[k // 8, pl.ds(k % 8, tt, stride=8), :] for k in range(LRU_WIDTH // LANE)], axis=1)


def _lru_bwd_kernel(xc_ref, vec_ref, wa_ref, wx_ref, hb_ref, a_ref, u_ref, h_ref, carry_ref, *, tile):
    hb_ref[0] = _lru_scan_tile(xc_ref, vec_ref, wa_ref, wx_ref, a_ref, u_ref, h_ref, carry_ref,
                               tile=tile, reverse=True)


def _lru_fwd_kernel(xc_ref, vec_ref, wa_ref, wx_ref, hb_ref, gate_ref, res_ref, wo_ref,
                    o_ref, a_ref, u_ref, h_ref, carry_ref, *, tile):
    h = _lru_scan_tile(xc_ref, vec_ref, wa_ref, wx_ref, a_ref, u_ref, h_ref, carry_ref, tile=tile, reverse=False)
    y = gate_ref[0] * (h + hb_ref[0])
    o_ref[0] = res_ref[0] + _dot(y.astype(BF16), wo_ref[...])


def _gate_bands(w):
    half = (0.5 * w).astype(BF16)
    bands = jnp.zeros((len(LRU_BAND_START), LRU_KWIN, LRU_NTILE), BF16)
    for c, k0 in enumerate(LRU_BAND_START):
        lo, hi = c * LRU_NTILE, min((c + 1) * LRU_NTILE, LRU_WIDTH)
        for n in range(lo // LRU_BLOCK, (hi - 1) // LRU_BLOCK + 1):
            r0 = n * LRU_BLOCK - k0
            assert 0 <= r0 and r0 + LRU_BLOCK <= LRU_KWIN
            c0, c1 = max(n * LRU_BLOCK, lo), min((n + 1) * LRU_BLOCK, hi)
            bands = bands.at[c, r0:r0 + LRU_BLOCK, c0 - lo:c1 - lo].set(
                half[n, :, c0 - n * LRU_BLOCK:c1 - n * LRU_BLOCK])
    return bands


def _lru_layer(x, g, w_in, conv_w, conv_b, w_a, b_a, w_x, b_x, lam, w_out):
    b, t, _ = x.shape
    c = LRU_WIDTH
    tile = TOK_TILE
    n_tiles = t // tile
    gate, xc = _lru_in(x.reshape(b * t, D_MODEL), g, w_in.astype(BF16), conv_w, conv_b, t)
    gate = gate.reshape(b, t, c)
    xc = xc.reshape(b, t, c)
    band_shape = (len(LRU_BAND_START), LRU_KWIN, LRU_NTILE)

    def scan_specs(reverse):
        return [pl.BlockSpec((1, tile, c), lambda i, s: (i, (n_tiles - 1 - s) if reverse else s, 0)),
                _resident((3, c)), _resident(band_shape), _resident(band_shape)]

    n_groups = -(-(c // LANE) // 8)
    slab_rows = pltpu.VMEM((n_groups, tile * 8, LANE), F32)
    scratch = [slab_rows, slab_rows, slab_rows, pltpu.VMEM((n_groups, 8, LANE), F32)]

    def direction_params(d):
        vec = jnp.stack([b_a[d], b_x[d], lam[d]]).astype(F32)
        return vec, _gate_bands(w_a[d]), _gate_bands(w_x[d])

    hb = pl.pallas_call(
        functools.partial(_lru_bwd_kernel, tile=tile),
        out_shape=jax.ShapeDtypeStruct((b, t, c), F32),
        grid=(b, n_tiles),
        in_specs=scan_specs(True),
        out_specs=pl.BlockSpec((1, tile, c), lambda i, s: (i, n_tiles - 1 - s, 0)),
        scratch_shapes=scratch,
        compiler_params=_cparams("arbitrary", "arbitrary"),
        name="lru_bwd",
    )(xc, *direction_params(1))

    tok = lambda n: pl.BlockSpec((1, tile, n), lambda i, s: (i, s, 0))
    return pl.pallas_call(
        functools.partial(_lru_fwd_kernel, tile=tile),
        out_shape=jax.ShapeDtypeStruct((b, t, D_MODEL), F32),
        grid=(b, n_tiles),
        in_specs=scan_specs(False) + [tok(c), tok(c), tok(D_MODEL), _resident((c, D_MODEL))],
        out_specs=tok(D_MODEL),
        scratch_shapes=scratch,
        compiler_params=_cparams("arbitrary", "arbitrary"),
        name="lru_fwd",
    )(xc, *direction_params(0), hb, gate, x, w_out.astype(BF16)).reshape(b * t, D_MODEL)


def _rope_lanes(v, cos, sin):
    return v * cos + pltpu.roll(v, LANE // 2, 1) * sin


def _mla_proj_kernel(x_ref, g_ref, wdq_ref, gq_ref, wuq_ref, wdkv_ref, gkv_ref, wukv_ref, cos_ref, sin_ref,
                     qn_ref, qr_ref, kn_ref, kr_ref, v_ref):
    scale = (MLA_NOPE + MLA_ROPE) ** -0.5 * LOG2E
    nq = MLA_HEADS * MLA_NOPE
    xn = _rms(x_ref[0], g_ref[...]).astype(BF16)
    cos = cos_ref[...]
    sin = sin_ref[...]
    cq = _rms(_dot(xn, wdq_ref[...]), gq_ref[...]).astype(BF16)
    q = _dot(cq, wuq_ref[...])
    qn_ref[0] = (q[:, 0:nq] * scale).astype(BF16)
    for p in range(MLA_HEADS // 2):
        sl = slice(nq + p * LANE, nq + (p + 1) * LANE)
        qr_ref[0, :, p * LANE:(p + 1) * LANE] = (_rope_lanes(q[:, sl], cos, sin) * scale).astype(BF16)
    kva = _dot(xn, wdkv_ref[...])
    kr_ref[0] = _rope_lanes(kva[:, MLA_KV_RANK:MLA_KV_RANK + LANE], cos, sin).astype(BF16)
    ckv = _rms(kva[:, 0:MLA_KV_RANK], gkv_ref[...]).astype(BF16)
    kv = _dot(ckv, wukv_ref[...])
    kn_ref[0] = kv[:, 0:nq].astype(BF16)
    v_ref[0] = kv[:, nq:2 * nq].astype(BF16)


def _mla_attn_kernel(qn_ref, qr_ref, kn_ref, kr_ref, v_ref, o_ref,
                     ql_ref, s_ref, p_ref, mxb_ref, m_ref, acc_ref, *, seq):
    tq, tk, ch = MLA_TQ, MLA_TK, MLA_CH
    rows = 2 * tq
    n = seq // tk
    assert n >= 2 and n % 2 == 0
    lane = lax.broadcasted_iota(jnp.int32, (1, 2 * LANE), 1)
    in_a = (lane < MLA_NOPE) | ((lane >= LANE) & (lane < LANE + MLA_ROPE))
    in_b = ((lane >= MLA_NOPE) & (lane < LANE)) | ((lane >= LANE + MLA_ROPE) & (lane < LANE + 2 * MLA_ROPE))
    lo = lax.broadcasted_iota(jnp.int32, (1, LANE), 1) < MLA_V

    def lanes(pp):
        return slice(pp * LANE, (pp + 1) * LANE)

    for pp in range(MLA_PAIRS):
        qp = jnp.concatenate([qn_ref[0, :, lanes(pp)], qr_ref[0, :, lanes(pp)]], axis=1)
        zero = jnp.zeros_like(qp)
        ql_ref[pp, 0:tq, :] = jnp.where(in_a, qp, zero)
        ql_ref[pp, tq:rows, :] = jnp.where(in_b, qp, zero)

    def key_rows(kt):
        return pl.ds(kt * tk if isinstance(kt, int) else pl.multiple_of(kt * tk, tk), tk)

    def qk(pp, kt, slot):
        kk = jnp.concatenate([kn_ref[0, key_rows(kt), lanes(pp)], kr_ref[0, key_rows(kt), :]], axis=1)
        s = _dot_nt(ql_ref[pp], kk)
        s_ref[slot] = s
        mxb_ref[slot] = jnp.broadcast_to(jnp.max(s, axis=1, keepdims=True), (rows, LANE))

    def pv(pp, kt, slot):
        v = v_ref[0, key_rows(kt), lanes(pp)]
        return _dot(p_ref[slot], jnp.concatenate([v, jnp.ones_like(v)], axis=1))

    def finish(pp, o_last):
        total = acc_ref[pp] + o_last
        o = total[:, 0:LANE] / total[:, LANE:2 * LANE]
        o_ref[0, :, lanes(pp)] = jnp.where(lo, o[:tq], o[tq:]).astype(BF16)

    def stage(pp, kt, slot, nxt):
        first = isinstance(kt, int) and kt == 0
        if nxt is not None:
            qk(nxt[0], nxt[1], 1 - slot)
        o_prev = None
        if not first:
            o_prev = pv(pp, kt - 1, 1 - slot)
        elif pp > 0:
            finish(pp - 1, pv(pp - 1, n - 1, 1 - slot))
        if first:
            m_new = mxb_ref[slot]
        else:
            m_old = m_ref[pp]
            m_new = jnp.maximum(m_old, mxb_ref[slot])
            alpha = jnp.exp2(m_old - m_new)
        m_ref[pp] = m_new
        for c in range(rows // ch):
            r = slice(c * ch, (c + 1) * ch)
            ps = [jnp.exp2(s_ref[slot, r, g * LANE:(g + 1) * LANE] - m_new[r, :]) for g in range(tk // LANE)]
            p_ref[slot, r, :] = jnp.concatenate(ps, axis=1).astype(BF16)
            for half in (slice(0, LANE), slice(LANE, 2 * LANE)):
                if first:
                    acc_ref[pp, r, half] = jnp.zeros((ch, LANE), F32)
                else:
                    acc_ref[pp, r, half] = alpha[r, :] * (acc_ref[pp, r, half] + o_prev[r, half])

    tiles = [(pp, kt) for pp in range(MLA_PAIRS) for kt in range(n)]
    qk(0, 0, 0)
    for i, (pp, kt) in enumerate(tiles):
        stage(pp, kt, kt % 2, tiles[i + 1] if i + 1 < len(tiles) else None)
    finish(MLA_PAIRS - 1, pv(MLA_PAIRS - 1, n - 1, (n - 1) % 2))


def _mla_weights(w_uq, w_dkv, w_ukv):
    h, nope, rope = MLA_HEADS, MLA_NOPE, MLA_ROPE
    half = rope // 2
    wq = w_uq.reshape(MLA_Q_RANK, h, nope + rope)
    wq_nope = wq[:, :, :nope].reshape(MLA_Q_RANK, h * nope)
    wq_rope = wq[:, :, nope:].reshape(MLA_Q_RANK, h // 2, 2, rope)
    wq_swap = jnp.concatenate([wq_rope[..., half:], wq_rope[..., :half]], axis=-1)
    wq_rope = jnp.concatenate([wq_rope.reshape(MLA_Q_RANK, h // 2, 2 * rope),
                               wq_swap.reshape(MLA_Q_RANK, h // 2, 2 * rope)], axis=-1)
    wuq = jnp.concatenate([wq_nope, wq_rope.reshape(MLA_Q_RANK, h // 2 * LANE)], axis=1)
    w_kr = w_dkv[:, MLA_KV_RANK:]
    w_krs = jnp.concatenate([w_kr[:, half:], w_kr[:, :half]], axis=1)
    wdkv = jnp.concatenate([w_dkv[:, :MLA_KV_RANK], w_kr, w_kr, w_krs, w_krs], axis=1)
    wkv = w_ukv.reshape(MLA_KV_RANK, h, nope + MLA_V)
    wukv = jnp.concatenate([wkv[:, :, :nope].reshape(MLA_KV_RANK, h * nope),
                            wkv[:, :, nope:].reshape(MLA_KV_RANK, h * MLA_V)], axis=1)
    return wuq.astype(BF16), wdkv.astype(BF16), wukv.astype(BF16)


def _rope_tables(t):
    pos = jnp.arange(t, dtype=F32)
    inv = ROPE_THETA ** (-jnp.arange(0, MLA_ROPE, 2, dtype=F32) / MLA_ROPE)
    ang = pos[:, None] * inv[None, :]
    cos, sin = jnp.cos(ang), jnp.sin(ang)
    pad = jnp.zeros((t, LANE // 2), F32)
    return (jnp.concatenate([cos, cos, cos, cos, pad], axis=1),
            jnp.concatenate([-sin, sin, -sin, sin, pad], axis=1))


def _mla_layer(x, g, w_dq, g_q, w_uq, w_dkv, g_kv, w_ukv, w_o):
    b, t, _ = x.shape
    tm = TOK_TILE
    wuq, wdkv, wukv = _mla_weights(w_uq, w_dkv, w_ukv)
    cos, sin = _rope_tables(t)
    wide = MLA_HEADS * MLA_NOPE
    tok = lambda n: pl.BlockSpec((1, tm, n), lambda i, s: (i, s, 0))
    big = jax.ShapeDtypeStruct((b, t, wide), BF16)
    qn, qr, kn, kr, v = pl.pallas_call(
        _mla_proj_kernel,
        out_shape=(big, big, big, jax.ShapeDtypeStruct((b, t, LANE), BF16), big),
        grid=(b, t // tm),
        in_specs=[tok(D_MODEL), _resident((1, D_MODEL)),
                  _resident((D_MODEL, MLA_Q_RANK)), _resident((1, MLA_Q_RANK)), _resident(wuq.shape),
                  _resident(wdkv.shape), _resident((1, MLA_KV_RANK)), _resident(wukv.shape),
                  pl.BlockSpec((tm, LANE), lambda i, s: (s, 0)), pl.BlockSpec((tm, LANE), lambda i, s: (s, 0))],
        out_specs=(tok(wide), tok(wide), tok(wide), tok(LANE), tok(wide)),
        compiler_params=_cparams("parallel", "parallel"),
        name="mla_proj",
    )(x, g.reshape(1, D_MODEL), w_dq.astype(BF16), g_q.reshape(1, MLA_Q_RANK), wuq,
      wdkv, g_kv.reshape(1, MLA_KV_RANK), wukv, cos, sin)

    width = MLA_PAIRS * LANE
    tk = MLA_TK
    rows = 2 * MLA_TQ
    q_spec = pl.BlockSpec((1, MLA_TQ, width), lambda i, p, s: (i, s, p))
    kv_spec = pl.BlockSpec((1, t, width), lambda i, p, s: (i, 0, p))
    o = pl.pallas_call(
        functools.partial(_mla_attn_kernel, seq=t),
        out_shape=jax.ShapeDtypeStruct((b, t, MLA_HEADS * MLA_V), BF16),
        grid=(b, MLA_HEADS // 2 // MLA_PAIRS, t // MLA_TQ),
        in_specs=[q_spec, q_spec, kv_spec,
                  pl.BlockSpec((1, t, LANE), lambda i, p, s: (i, 0, 0)), kv_spec],
        out_specs=q_spec,
        scratch_shapes=[pltpu.VMEM((MLA_PAIRS, rows, 2 * LANE), BF16),
                        pltpu.VMEM((2, rows, tk), F32),
                        pltpu.VMEM((2, rows, tk), BF16),
                        pltpu.VMEM((2, rows, LANE), F32),
                        pltpu.VMEM((MLA_PAIRS, rows, LANE), F32),
                        pltpu.VMEM((MLA_PAIRS, rows, 2 * LANE), F32)],
        compiler_params=_cparams("parallel", "parallel", "parallel"),
        name="mla_attn",
    )(qn, qr, kn, kr, v)
    return o.reshape(b * t, MLA_HEADS * MLA_V), w_o.astype(BF16)


def kernel(x_prompt, x_sample, norm_mix, norm_ffn, norm_final, na_w_qkv, na_rpb, na_w_o, lru_w_in, lru_conv_w, lru_conv_b, lru_w_a, lru_b_a, lru_w_x, lru_b_x, lru_lam, lru_w_out, mla_w_dq, mla_g_q, mla_w_uq, mla_w_dkv, mla_g_kv, mla_w_ukv, mla_w_o, ffn_w_gate, ffn_w_up, ffn_w_down):
    wg = ffn_w_gate.astype(BF16)
    wu = ffn_w_up.astype(BF16)
    wd = ffn_w_down.astype(BF16)

    def run(x):
        b, t, _ = x.shape
        x = x.reshape(b * t, D_MODEL)
        for i in range(DEPTH):
            j, m = divmod(i, N_MIXERS)
            x3 = x.reshape(b, t, D_MODEL)
            attn = w_o = None
            if m == 0:
                attn, w_o = _na_layer(x3, norm_mix[i], na_w_qkv[j], na_rpb[j], na_w_o[j])
            elif m == 1:
                x = _lru_layer(x3, norm_mix[i], lru_w_in[j], lru_conv_w[j], lru_conv_b[j], lru_w_a[j], lru_b_a[j],
                               lru_w_x[j], lru_b_x[j], lru_lam[j], lru_w_out[j])
            else:
                attn, w_o = _mla_layer(x3, norm_mix[i], mla_w_dq[j], mla_g_q[j], mla_w_uq[j], mla_w_dkv[j],
                                       mla_g_kv[j], mla_w_ukv[j], mla_w_o[j])
            x = _ffn(x, norm_ffn[i], i, wg, wu, wd, norm_final if i == DEPTH - 1 else None, attn, w_o)
        return x.reshape(b, t, D_MODEL)

    return run(x_prompt), run(x_sample)
```
